```python
import math
import jax
import jax.numpy as jnp
from jax import lax
import numpy as np

D_MODEL = 4096
BATCH = 4
SEQ = 2048
DEPTH = 4
DEC_BATCH = 8
DEC_SEQ = 32
PAST_LEN = 1024

CHUNK = 64
Q_BLOCK = 128
HEAD_DIM = 128
D_MIX = D_MODEL
N_HEADS_A = D_MIX // 2 // HEAD_DIM
N_IDX_HEADS = 16
IDX_DIM = 64
TOPK_MAX = 256
N_HEADS_B = D_MIX // 2 // HEAD_DIM
SUB_DIM = HEAD_DIM // 2
NUM_BUCKETS = 32
MAX_DISTANCE = 128
N_GROUPS = 4
EXPERTS_PER_GROUP = 4
N_EXPERTS = N_GROUPS * EXPERTS_PER_GROUP
TOPK_IN_GROUP = 2
D_FF_EXPERT = D_MODEL // 4
LN_EPS = 1e-5
RMS_EPS = 1e-5
DEEPNORM_ALPHA = (2.0 * DEPTH) ** 0.25
DEEPNORM_BETA = (8.0 * DEPTH) ** -0.25

IN_WIDTHS = (N_HEADS_A * HEAD_DIM, HEAD_DIM, HEAD_DIM, N_IDX_HEADS * IDX_DIM, IDX_DIM, N_IDX_HEADS,
             N_HEADS_B * 2 * SUB_DIM, N_HEADS_B * 2 * SUB_DIM, N_HEADS_B * 2 * SUB_DIM)
IN_SCALES = (1.0, 1.0, DEEPNORM_BETA, 1.0, 1.0, 1.0, 1.0, 1.0, DEEPNORM_BETA)
IN_SPLITS = tuple(int(s) for s in np.cumsum(IN_WIDTHS)[:-1])
N_IN = int(sum(IN_WIDTHS))

kernel_name = 'hymba_dsa_diffattn_hiermoe_deepnorm_stream'


def _layer_norm(x, g, b):
    xf = x.astype(jnp.float32)
    mu = jnp.mean(xf, axis=-1, keepdims=True)
    var = jnp.mean(jnp.square(xf - mu), axis=-1, keepdims=True)
    return ((xf - mu) * lax.rsqrt(var + LN_EPS) * g + b).astype(x.dtype)


def _t5_bucket(rel):
    nb = NUM_BUCKETS // 2
    ret = jnp.where(rel > 0, nb, 0)
    n = jnp.abs(rel)
    max_exact = nb // 2
    nf = jnp.maximum(n, 1).astype(jnp.float32)
    large = max_exact + (jnp.log(nf / max_exact) / math.log(MAX_DISTANCE / max_exact)
                         * (nb - max_exact)).astype(jnp.int32)
    large = jnp.minimum(large, nb - 1)
    return (ret + jnp.where(n < max_exact, n, large)).astype(jnp.int32)


def _admissible(q_pos, k_pos):
    return (k_pos // CHUNK) <= (q_pos // CHUNK)


def _dsa_attend(q, q_idx, w_idx, k, v, k_idx, q_pos, k_pos, bias_table, topk):
    f32 = jnp.float32
    adm = _admissible(q_pos[:, None], k_pos[None, :])
    dots = jnp.einsum('bthd,bld->bthl', q_idx, k_idx, preferred_element_type=f32) * (IDX_DIM ** -0.5)
    w = w_idx.astype(f32) * (N_IDX_HEADS ** -0.5)
    score = jnp.einsum('bth,bthl->btl', w, jax.nn.relu(dots))
    score = jnp.where(adm[None], score, -jnp.inf)
    _, idx = lax.top_k(score, topk)
    gather = jax.vmap(lambda arr, ix: arr[ix])
    k_sel = gather(k, idx)
    v_sel = gather(v, idx)
    sel_pos = k_pos[idx]
    qp = q_pos[None, :, None]
    sel_adm = _admissible(qp, sel_pos)
    bias = bias_table[_t5_bucket(sel_pos - qp)].transpose(0, 3, 1, 2)
    logits = jnp.einsum('bthd,btkd->bhtk', q, k_sel, preferred_element_type=f32) * (HEAD_DIM ** -0.5) + bias
    logits = jnp.where(sel_adm[:, None], logits, -jnp.inf)
    p = jax.nn.softmax(logits, axis=-1).astype(v.dtype)
    return jnp.einsum('bhtk,btkd->bthd', p, v_sel)


def _diff_attend(q, k, v, q_pos, k_pos, bias_table, lam, lam_init, subln_g):
    f32 = jnp.float32
    adm = _admissible(q_pos[:, None], k_pos[None, :])
    bias = bias_table[_t5_bucket(k_pos[None, :] - q_pos[:, None])].transpose(2, 0, 1)
    logits = jnp.einsum('bthcd,blhcd->bchtl', q, k, preferred_element_type=f32) * (SUB_DIM ** -0.5)
    logits = jnp.where(adm[None, None, None], logits + bias[None, None], -jnp.inf)
    p = jax.nn.softmax(logits, axis=-1)
    attn = (p[:, 0] - lam * p[:, 1]).astype(v.dtype)
    o = jnp.einsum('bhtl,blhd->bthd', attn, v)
    of = o.astype(f32)
    of = of * lax.rsqrt(jnp.mean(of * of, axis=-1, keepdims=True) + RMS_EPS) * subln_g * (1.0 - lam_init)
    return of.astype(o.dtype)


def _over_query_blocks(fn, q_inputs, q_pos):
    seq = q_inputs[0].shape[1]

    def body(start):
        blk = [lax.dynamic_slice_in_dim(a, start, Q_BLOCK, axis=1) for a in q_inputs]
        return fn(*blk, lax.dynamic_slice_in_dim(q_pos, start, Q_BLOCK, axis=0))

    out = lax.map(body, jnp.arange(0, seq, Q_BLOCK, dtype=jnp.int32))
    out = jnp.moveaxis(out, 0, 1)
    return out.reshape((out.shape[0], seq) + out.shape[3:])


def _mixer(h, past, q_pos, k_pos, topk, block_queries, w_in, w_out, lam, lam_init, subln_g, rel_bias):
    b, t, _ = h.shape
    proj = jnp.einsum('btd,dn->btn', h, w_in)
    qa, ka, va, qi, ki, wi, qb, kb, vb = jnp.split(proj, IN_SPLITS, axis=-1)
    qa = qa.reshape(b, t, N_HEADS_A, HEAD_DIM)
    qi = qi.reshape(b, t, N_IDX_HEADS, IDX_DIM)
    qb = qb.reshape(b, t, N_HEADS_B, 2, SUB_DIM)
    kb = kb.reshape(b, t, N_HEADS_B, 2 * SUB_DIM)
    vb = vb.reshape(b, t, N_HEADS_B, 2 * SUB_DIM)
    new_rows = (ka, va, ki, kb, vb)
    if past is None:
        ka_all, va_all, ki_all, kb_all, vb_all = new_rows
    else:
        ka_all, va_all, ki_all, kb_all, vb_all = [jnp.concatenate([p_, n_], axis=1) for p_, n_ in zip(past, new_rows)]
    kb_all = kb_all.reshape(b, kb_all.shape[1], N_HEADS_B, 2, SUB_DIM)
    bias_a = rel_bias[:, :N_HEADS_A]
    bias_b = rel_bias[:, N_HEADS_A:]

    def mix_a(q_, qi_, wi_, qp):
        return _dsa_attend(q_, qi_, wi_, ka_all, va_all, ki_all, qp, k_pos, bias_a, topk)

    def mix_b(q_, qp):
        return _diff_attend(q_, kb_all, vb_all, qp, k_pos, bias_b, lam, lam_init, subln_g)

    if block_queries:
        oa = _over_query_blocks(mix_a, (qa, qi, wi), q_pos)
        ob = _over_query_blocks(mix_b, (qb,), q_pos)
    else:
        oa = mix_a(qa, qi, wi, q_pos)
        ob = mix_b(qb, q_pos)
    o = jnp.concatenate([oa.reshape(b, t, -1), ob.reshape(b, t, -1)], axis=-1)
    return jnp.einsum('btn,nd->btd', o, w_out), new_rows


def _hier_moe(h, w_rg, w_re, w1, w3, w2):
    f32 = jnp.float32
    g_prob = jax.nn.softmax(jnp.einsum('btd,dg->btg', h, w_rg, preferred_element_type=f32), axis=-1)
    g_val, g_idx = lax.top_k(g_prob, 1)
    e_logits = jnp.einsum('btd,dn->btn', h, w_re, preferred_element_type=f32)
    e_logits = e_logits.reshape(h.shape[:2] + (N_GROUPS, EXPERTS_PER_GROUP))
    e_logits = jnp.einsum('btg,btge->bte', jax.nn.one_hot(g_idx[..., 0], N_GROUPS, dtype=f32), e_logits)
    e_val, e_idx = lax.top_k(e_logits, TOPK_IN_GROUP)
    weights = jax.nn.softmax(e_val, axis=-1) * g_val
    expert_id = g_idx * EXPERTS_PER_GROUP + e_idx
    gates = jnp.einsum('btk,btke->bte', weights,
                       jax.nn.one_hot(expert_id, N_EXPERTS, dtype=f32)).astype(h.dtype)
    y = jnp.zeros_like(h)
    for e in range(N_EXPERTS):
        a = jax.nn.silu(h @ w1[e]) * (h @ w3[e])
        y = y + gates[..., e:e + 1] * (a @ w2[e])
    return y


def _trunk(x, caches, q_pos, k_pos, topk, block_queries, rel_bias, w_in, w_out, lambda_q1, lambda_k1,
           lambda_q2, lambda_k2, subln_g, ln1_g, ln1_b, ln2_g, ln2_b, w_route_group, w_route_expert, w1, w3, w2):
    f32 = jnp.float32
    rows = []
    for l in range(DEPTH):
        lam_init = 0.8 - 0.6 * math.exp(-0.3 * l)
        lam = (jnp.exp(jnp.sum(lambda_q1[l].astype(f32) * lambda_k1[l].astype(f32)))
               - jnp.exp(jnp.sum(lambda_q2[l].astype(f32) * lambda_k2[l].astype(f32))) + lam_init)
        past = None if caches is None else tuple(c[l] for c in caches)
        a, new_rows = _mixer(x, past, q_pos, k_pos, topk, block_queries, w_in[l], w_out[l], lam, lam_init,
                             subln_g[l], rel_bias)
        x = _layer_norm(DEEPNORM_ALPHA * x + a, ln1_g[l], ln1_b[l])
        f = _hier_moe(x, w_route_group[l], w_route_expert[l], w1[l], w3[l], w2[l])
        x = _layer_norm(DEEPNORM_ALPHA * x + f, ln2_g[l], ln2_b[l])
        rows.append(new_rows)
    stacked = tuple(jnp.stack(r, axis=0) for r in zip(*rows))
    return x, stacked


def setup_inputs(seed: int = 0) -> dict:
    key = jax.random.key(seed)
    ks = jax.random.split(key, 24)
    f32 = jnp.float32

    def nrm(k, shape, s):
        return jax.random.normal(k, shape, f32) * s

    col_scale = jnp.asarray(np.concatenate([np.full((w,), s, np.float32) for w, s in zip(IN_WIDTHS, IN_SCALES)]))
    return {
        'x_prompt': nrm(ks[0], (BATCH, SEQ, D_MODEL), 1.0),
        'x_sample': nrm(ks[1], (DEC_BATCH, DEC_SEQ, D_MODEL), 1.0),
        'cache_ka': nrm(ks[2], (DEPTH, DEC_BATCH, PAST_LEN, HEAD_DIM), 1.0),
        'cache_va': nrm(ks[3], (DEPTH, DEC_BATCH, PAST_LEN, HEAD_DIM), DEEPNORM_BETA),
        'cache_kidx': nrm(ks[4], (DEPTH, DEC_BATCH, PAST_LEN, IDX_DIM), 1.0),
        'cache_kb': nrm(ks[5], (DEPTH, DEC_BATCH, PAST_LEN, N_HEADS_B, 2 * SUB_DIM), 1.0),
        'cache_vb': nrm(ks[6], (DEPTH, DEC_BATCH, PAST_LEN, N_HEADS_B, 2 * SUB_DIM), DEEPNORM_BETA),
        'rel_bias': nrm(ks[7], (NUM_BUCKETS, N_HEADS_A + N_HEADS_B), 0.2),
        'w_in': nrm(ks[8], (DEPTH, D_MODEL, N_IN), D_MODEL ** -0.5) * col_scale,
        'w_out': nrm(ks[9], (DEPTH, D_MIX, D_MODEL), DEEPNORM_BETA * D_MIX ** -0.5),
        'lambda_q1': nrm(ks[10], (DEPTH, SUB_DIM), 0.1),
        'lambda_k1': nrm(ks[11], (DEPTH, SUB_DIM), 0.1),
        'lambda_q2': nrm(ks[12], (DEPTH, SUB_DIM), 0.1),
        'lambda_k2': nrm(ks[13], (DEPTH, SUB_DIM), 0.1),
        'subln_g': 1.0 + nrm(ks[14], (DEPTH, 2 * SUB_DIM), 0.01),
        'ln1_g': 1.0 + nrm(ks[15], (DEPTH, D_MODEL), 0.01),
        'ln1_b': nrm(ks[16], (DEPTH, D_MODEL), 0.01),
        'ln2_g': 1.0 + nrm(ks[17], (DEPTH, D_MODEL), 0.01),
        'ln2_b': nrm(ks[18], (DEPTH, D_MODEL), 0.01),
        'w_route_group': nrm(ks[19], (DEPTH, D_MODEL, N_GROUPS), D_MODEL ** -0.5),
        'w_route_expert': nrm(ks[20], (DEPTH, D_MODEL, N_GROUPS * EXPERTS_PER_GROUP), D_MODEL ** -0.5),
        'w1': nrm(ks[21], (DEPTH, N_EXPERTS, D_MODEL, D_FF_EXPERT), DEEPNORM_BETA * D_MODEL ** -0.5),
        'w3': nrm(ks[22], (DEPTH, N_EXPERTS, D_MODEL, D_FF_EXPERT), DEEPNORM_BETA * D_MODEL ** -0.5),
        'w2': nrm(ks[23], (DEPTH, N_EXPERTS, D_FF_EXPERT, D_MODEL), DEEPNORM_BETA * D_FF_EXPERT ** -0.5),
    }


def reference(x_prompt, x_sample, cache_ka, cache_va, cache_kidx, cache_kb, cache_vb, rel_bias, w_in, w_out,
              lambda_q1, lambda_k1, lambda_q2, lambda_k2, subln_g, ln1_g, ln1_b, ln2_g, ln2_b,
              w_route_group, w_route_expert, w1, w3, w2):
    seq = x_prompt.shape[1]
    dec_seq = x_sample.shape[1]
    past_len = cache_ka.shape[2]
    p_pos = jnp.arange(seq, dtype=jnp.int32)
    y_prompt, p_rows = _trunk(x_prompt, None, p_pos, p_pos, min(TOPK_MAX, seq // 4), True,
                              rel_bias, w_in, w_out, lambda_q1, lambda_k1, lambda_q2, lambda_k2, subln_g,
                              ln1_g, ln1_b, ln2_g, ln2_b, w_route_group, w_route_expert, w1, w3, w2)
    s_q_pos = past_len + jnp.arange(dec_seq, dtype=jnp.int32)
    s_k_pos = jnp.arange(past_len + dec_seq, dtype=jnp.int32)
    y_sample, s_rows = _trunk(x_sample, (cache_ka, cache_va, cache_kidx, cache_kb, cache_vb), s_q_pos, s_k_pos,
                              min(TOPK_MAX, (past_len + dec_seq) // 4), False,
                              rel_bias, w_in, w_out, lambda_q1, lambda_k1, lambda_q2, lambda_k2, subln_g,
                              ln1_g, ln1_b, ln2_g, ln2_b, w_route_group, w_route_expert, w1, w3, w2)
    p_ka, p_va, p_kidx, p_kb, p_vb = p_rows
    s_ka, s_va, s_kidx, s_kb, s_vb = s_rows
    return (y_prompt, y_sample, p_ka, p_va, p_kidx, p_kb, p_vb, s_ka, s_va, s_kidx, s_kb, s_vb)
```

```python
import functools
import math

import numpy as np
import jax
import jax.numpy as jnp
from jax import lax
from jax.experimental import pallas as pl
from jax.experimental.pallas import tpu as pltpu

CHUNK = 64
HEAD_DIM = 128
N_HEADS = 16
N_IDX_HEADS = 16
IDX_DIM = 64
SUB_DIM = 64
TOPK_MAX = 256
NUM_BUCKETS = 32
MAX_DISTANCE = 128
N_GROUPS = 4
EXPERTS_PER_GROUP = 4
N_EXPERTS = 16
LN_EPS = 1e-5
RMS_EPS = 1e-5

LANES = 128
QB = 128
WIN = 2 * QB
NEG = -1e30
INT_MIN = -2147483648
VMEM_LIMIT = 56 * 1024 * 1024

F32 = jnp.float32
BF16 = jnp.bfloat16


def _cparams(*sem):
    return pltpu.CompilerParams(dimension_semantics=sem, vmem_limit_bytes=VMEM_LIMIT)


def _dot_nt(a, b):
    return lax.dot_general(a, b, (((1,), (1,)), ((), ())), preferred_element_type=F32)


def _dot(a, b):
    return jnp.dot(a, b, preferred_element_type=F32)


def _bucket_np(rel):
    nb = NUM_BUCKETS // 2
    ret = np.where(rel > 0, nb, 0)
    n = np.abs(rel)
    max_exact = nb // 2
    nf = np.maximum(n, 1).astype(np.float32)
    frac = np.log(nf / np.float32(max_exact)) / np.float32(math.log(MAX_DISTANCE / max_exact))
    large = max_exact + (frac.astype(np.float32) * np.float32(nb - max_exact)).astype(np.int32)
    large = np.minimum(large, nb - 1)
    return (ret + np.where(n < max_exact, n, large)).astype(np.int32)


def _prompt_bias_tiles(rel_bias):
    ql = np.arange(QB)[:, None]
    c = np.arange(WIN)[None, :]
    rel = c - QB - ql
    bucket = _bucket_np(rel)
    kl = c - QB
    adm = (c < QB) | ((kl // CHUNK) <= (ql // CHUNK))
    tiles = jnp.transpose(rel_bias[bucket], (2, 0, 1))
    tiles = jnp.where(jnp.asarray(adm)[None], tiles, NEG)
    far = rel_bias[int(_bucket_np(np.array([-(QB + 1)]))[0])]
    return tiles.astype(F32), far.astype(F32)


def _sample_bias(rel_bias, past_len, dec_seq):
    q = past_len + np.arange(dec_seq)[:, None]
    k = np.arange(past_len + dec_seq)[None, :]
    return jnp.transpose(rel_bias[_bucket_np(k - q)], (2, 0, 1)).astype(F32)


def _proj_body(l_ref, x_ref, w_ref, *o_refs):
    acc = _dot(x_ref[...], w_ref[...])
    o_refs[0][...] = acc.astype(BF16)
    if len(o_refs) > 1:
        o_refs[1][...] = acc


def _proj(lidx, xb, w, tm, tn, with_f32):
    m, d = xb.shape
    n = w.shape[-1]
    out_shape = [jax.ShapeDtypeStruct((m, n), BF16)]
    out_specs = [pl.BlockSpec((tm, tn), lambda i, j, l: (i, j))]
    if with_f32:
        out_shape.append(jax.ShapeDtypeStruct((m, n), F32))
        out_specs.append(pl.BlockSpec((tm, tn), lambda i, j, l: (i, j)))
    return pl.pallas_call(
        _proj_body,
        grid_spec=pltpu.PrefetchScalarGridSpec(
            num_scalar_prefetch=1, grid=(m // tm, n // tn),
            in_specs=[pl.BlockSpec((tm, d), lambda i, j, l: (i, 0)),
                      pl.BlockSpec((None, d, tn), lambda i, j, l: (l[0], 0, j))],
            out_specs=out_specs),
        out_shape=out_shape,
        compiler_params=_cparams("parallel", "arbitrary"),
        name="in_proj_f32" if with_f32 else "in_proj",
    )(lidx, xb, w)


def _sortable(score):
    bits = pltpu.bitcast(score + 0.0, jnp.int32)
    return jnp.where(bits < 0, bits ^ jnp.int32(0x7FFFFFFF), bits)


def _kth_largest(keys, k):
    t = keys[0].shape[0]

    def step(it, thr):
        bit = jnp.left_shift(jnp.int32(1), 31 - it)
        cand = thr + bit
        cnt = jnp.zeros((t, 1), F32)
        for kk in keys:
            cnt = cnt + jnp.sum(jnp.where(kk >= cand, 1.0, 0.0), axis=-1, keepdims=True)
        return jnp.where(cnt >= k, cand, thr)

    return lax.fori_loop(0, 32, step, jnp.full((t, 1), INT_MIN, jnp.int32))


def _index_scores(qi, wi, kparts):
    t = qi.shape[0]
    lane = lax.broadcasted_iota(jnp.int32, (t, LANES), 1)
    w = wi * (N_IDX_HEADS ** -0.5 * IDX_DIM ** -0.5)
    acc = [jnp.zeros((t, kp.shape[0]), F32) for kp in kparts]
    for p in range(N_IDX_HEADS // 2):
        qp = qi[:, p * LANES:(p + 1) * LANES]
        halves = (jnp.where(lane < IDX_DIM, qp, jnp.zeros_like(qp)),
                  jnp.where(lane >= IDX_DIM, qp, jnp.zeros_like(qp)))
        for s, qh in enumerate(halves):
            h = 2 * p + s
            wh = w[:, h:h + 1]
            for j, kp in enumerate(kparts):
                acc[j] = acc[j] + jnp.maximum(_dot_nt(qh, kp), 0.0) * wh
    return acc


def _softmax_parts(parts):
    m = parts[0].max(axis=-1, keepdims=True)
    for s in parts[1:]:
        m = jnp.maximum(m, s.max(axis=-1, keepdims=True))
    es = [jnp.exp(s - m) for s in parts]
    l = es[0].sum(axis=-1, keepdims=True)
    for e in es[1:]:
        l = l + e.sum(axis=-1, keepdims=True)
    return es, 1.0 / l


def _lane_half_masks(q):
    lane = lax.broadcasted_iota(jnp.int32, q.shape, 1)
    z = jnp.zeros_like(q)
    return jnp.where(lane < SUB_DIM, q, z), jnp.where(lane >= SUB_DIM, q, z)


def _lambda(lq1, lk1, lq2, lk2, lam_init):
    return (jnp.exp(jnp.sum(lq1 * lk1, axis=-1, keepdims=True))
            - jnp.exp(jnp.sum(lq2 * lk2, axis=-1, keepdims=True)) + lam_init)


def _sub_ln(o, g, lam_init):
    return o * lax.rsqrt(jnp.mean(o * o, axis=-1, keepdims=True) + RMS_EPS) * g * (1.0 - lam_init)


def _dsa_prompt_body(far_len, topk, cfar_ref, qa_ref, qi_ref, wi_ref, k_ref, v_ref, kx_ref, dwin_ref,
                     o_ref, mfar_ref, mwin_ref):
    i = pl.program_id(1)
    prev0 = pl.multiple_of(jnp.maximum(i - 1, 0) * QB, QB)
    diag0 = pl.multiple_of(i * QB, QB)

    def window(ref):
        return jnp.concatenate([ref[pl.ds(prev0, QB), :], ref[pl.ds(diag0, QB), :]], axis=0)

    s_far, s_win = _index_scores(qi_ref[...], wi_ref[...], [kx_ref[0:far_len, :], window(kx_ref)])
    col_far = lax.broadcasted_iota(jnp.int32, s_far.shape, 1)
    col_win = lax.broadcasted_iota(jnp.int32, s_win.shape, 1)
    ok_far = col_far < (i - 1) * QB
    row_win = lax.broadcasted_iota(jnp.int32, s_win.shape, 0)
    ok_win = ((col_win < QB + CHUNK) | (row_win >= CHUNK)) & ((col_win >= QB) | (i > 0))
    key_far = jnp.where(ok_far, _sortable(s_far), INT_MIN)
    key_win = jnp.where(ok_win, _sortable(s_win), INT_MIN)
    thr = _kth_largest([key_far, key_win], topk)
    mfar_ref[...] = jnp.where(ok_far & (key_far >= thr), 0.0, NEG)
    mwin_ref[...] = jnp.where(ok_win & (key_win >= thr), 0.0, NEG)

    k_far = k_ref[0:far_len, :]
    v_far = v_ref[0:far_len, :]
    k_win = window(k_ref)
    v_win = window(v_ref)
    scale = HEAD_DIM ** -0.5

    def head(h, carry):
        off = pl.multiple_of(h * HEAD_DIM, HEAD_DIM)
        q = qa_ref[:, pl.ds(off, HEAD_DIM)]
        lf = _dot_nt(q, k_far) * scale + cfar_ref[h] + mfar_ref[...]
        lw = _dot_nt(q, k_win) * scale + dwin_ref[h] + mwin_ref[...]
        (ef, ew), rl = _softmax_parts([lf, lw])
        o = (_dot(ef.astype(BF16), v_far) + _dot(ew.astype(BF16), v_win)) * rl
        o_ref[:, pl.ds(off, HEAD_DIM)] = o.astype(o_ref.dtype)
        return carry

    lax.fori_loop(0, N_HEADS, head, 0)


def _dsa_prompt(projq, pkv_b, pkv_f, dwin, cfar, batch, seq, cols):
    nq = seq // QB
    far_len = seq - WIN
    topk = min(TOPK_MAX, seq // 4)
    return pl.pallas_call(
        functools.partial(_dsa_prompt_body, far_len, topk),
        grid=(batch, nq),
        in_specs=[
            pl.BlockSpec(memory_space=pltpu.SMEM),
            pl.BlockSpec((QB, N_HEADS * HEAD_DIM), lambda b, i: (b * nq + i, cols["qa"])),
            pl.BlockSpec((QB, N_IDX_HEADS * IDX_DIM), lambda b, i: (b * nq + i, cols["qi"])),
            pl.BlockSpec((QB, LANES), lambda b, i: (b * nq + i, cols["wi"])),
            pl.BlockSpec((seq, HEAD_DIM), lambda b, i: (b, cols["ka"])),
            pl.BlockSpec((seq, HEAD_DIM), lambda b, i: (b, cols["va"])),
            pl.BlockSpec((seq, LANES), lambda b, i: (b, cols["kx"])),
            pl.BlockSpec((N_HEADS, QB, WIN), lambda b, i: (0, 0, 0)),
        ],
        out_specs=pl.BlockSpec((QB, N_HEADS * HEAD_DIM), lambda b, i: (b * nq + i, 0)),
        out_shape=jax.ShapeDtypeStruct((batch * seq, N_HEADS * HEAD_DIM), BF16),
        scratch_shapes=[pltpu.VMEM((QB, far_len), F32), pltpu.VMEM((QB, WIN), F32)],
        compiler_params=_cparams("parallel", "arbitrary"),
        name="dsa_prompt",
    )(cfar, projq, projq, pkv_f, pkv_b, pkv_b, pkv_b, dwin)


def _diff_prompt_body(far_len, l_ref, cfar_ref, laminit_ref, qb_ref, kb_ref, vb_ref, dwin_ref,
                      lq1_ref, lk1_ref, lq2_ref, lk2_ref, g_ref, o_ref):
    i = pl.program_id(1)
    prev0 = pl.multiple_of(jnp.maximum(i - 1, 0) * QB, QB)
    diag0 = pl.multiple_of(i * QB, QB)
    lam_init = laminit_ref[l_ref[0]]
    lam = _lambda(lq1_ref[...], lk1_ref[...], lq2_ref[...], lk2_ref[...], lam_init)
    col_far = lax.broadcasted_iota(jnp.int32, (QB, far_len), 1)
    col_win = lax.broadcasted_iota(jnp.int32, (QB, WIN), 1)
    m_far = jnp.where(col_far < (i - 1) * QB, 0.0, NEG)
    m_win = jnp.where((col_win >= QB) | (i > 0), 0.0, NEG)
    scale = SUB_DIM ** -0.5
    g = g_ref[...]

    def head(h, carry):
        off = pl.multiple_of(h * HEAD_DIM, HEAD_DIM)
        q1, q2 = _lane_half_masks(qb_ref[:, pl.ds(off, HEAD_DIM)])
        k_far = kb_ref[0:far_len, pl.ds(off, HEAD_DIM)]
        v_far = vb_ref[0:far_len, pl.ds(off, HEAD_DIM)]
        k_win = jnp.concatenate([kb_ref[pl.ds(prev0, QB), pl.ds(off, HEAD_DIM)],
                                 kb_ref[pl.ds(diag0, QB), pl.ds(off, HEAD_DIM)]], axis=0)
        v_win = jnp.concatenate([vb_ref[pl.ds(prev0, QB), pl.ds(off, HEAD_DIM)],
                                 vb_ref[pl.ds(diag0, QB), pl.ds(off, HEAD_DIM)]], axis=0)
        bias_far = m_far + cfar_ref[h]
        bias_win = m_win + dwin_ref[h]
        (e1f, e1w), r1 = _softmax_parts([_dot_nt(q1, k_far) * scale + bias_far,
                                         _dot_nt(q1, k_win) * scale + bias_win])
        (e2f, e2w), r2 = _softmax_parts([_dot_nt(q2, k_far) * scale + bias_far,
                                         _dot_nt(q2, k_win) * scale + bias_win])
        r2 = r2 * lam
        a_far = (e1f * r1 - e2f * r2).astype(BF16)
        a_win = (e1w * r1 - e2w * r2).astype(BF16)
        o = _dot(a_far, v_far) + _dot(a_win, v_win)
        o_ref[:, pl.ds(off, HEAD_DIM)] = _sub_ln(o, g, lam_init).astype(o_ref.dtype)
        return carry

    lax.fori_loop(0, N_HEADS, head, 0)


def _diff_prompt(lidx, projq, pkv_b, dwin, cfar, lam_inits, lq1, lk1, lq2, lk2, subg, batch, seq, cols):
    nq = seq // QB
    far_len = seq - WIN
    width = N_HEADS * HEAD_DIM
    vec = lambda n: pl.BlockSpec((None, 1, n), lambda b, i, l: (l[0], 0, 0))
    return pl.pallas_call(
        functools.partial(_diff_prompt_body, far_len),
        grid_spec=pltpu.PrefetchScalarGridSpec(
            num_scalar_prefetch=1, grid=(batch, nq),
            in_specs=[
                pl.BlockSpec(memory_space=pltpu.SMEM),
                pl.BlockSpec(memory_space=pltpu.SMEM),
                pl.BlockSpec((QB, width), lambda b, i, l: (b * nq + i, cols["qb"])),
                pl.BlockSpec((seq, width), lambda b, i, l: (b, cols["kb"])),
                pl.BlockSpec((seq, width), lambda b, i, l: (b, cols["vb"])),
                pl.BlockSpec((N_HEADS, QB, WIN), lambda b, i, l: (0, 0, 0)),
                vec(SUB_DIM), vec(SUB_DIM), vec(SUB_DIM), vec(SUB_DIM), vec(HEAD_DIM),
            ],
            out_specs=pl.BlockSpec((QB, width), lambda b, i, l: (b * nq + i, 0))),
        out_shape=jax.ShapeDtypeStruct((batch * seq, width), BF16),
        compiler_params=_cparams("parallel", "arbitrary"),
        name="diff_prompt",
    )(lidx, cfar, lam_inits, projq, pkv_b, pkv_b, dwin, lq1, lk1, lq2, lk2, subg)


def _dsa_sample_body(topk, l_ref, qa_ref, qi_ref, wi_ref, kn_ref, vn_ref, kxn_ref, kc_ref, vc_ref, kxc_ref,
                     bc_ref, bn_ref, o_ref, mc_ref, mn_ref):
    s_c, s_n = _index_scores(qi_ref[...], wi_ref[...], [kxc_ref[...].astype(BF16), kxn_ref[...]])
    key_c = _sortable(s_c)
    key_n = _sortable(s_n)
    thr = _kth_largest([key_c, key_n], topk)
    mc_ref[...] = jnp.where(key_c >= thr, 0.0, NEG)
    mn_ref[...] = jnp.where(key_n >= thr, 0.0, NEG)
    k_c = kc_ref[...].astype(BF16)
    v_c = vc_ref[...].astype(BF16)
    k_n = kn_ref[...]
    v_n = vn_ref[...]
    scale = HEAD_DIM ** -0.5

    def head(h, carry):
        off = pl.multiple_of(h * HEAD_DIM, HEAD_DIM)
        q = qa_ref[:, pl.ds(off, HEAD_DIM)]
        lc = _dot_nt(q, k_c) * scale + bc_ref[h] + mc_ref[...]
        ln = _dot_nt(q, k_n) * scale + bn_ref[h] + mn_ref[...]
        (ec, en), rl = _softmax_parts([lc, ln])
        o = (_dot(ec.astype(BF16), v_c) + _dot(en.astype(BF16), v_n)) * rl
        o_ref[:, pl.ds(off, HEAD_DIM)] = o.astype(o_ref.dtype)
        return carry

    lax.fori_loop(0, N_HEADS, head, 0)


def _dsa_sample(lidx, projq, pkv_b, pkv_f, cache_ka, cache_va, cache_kidx, bias_c, bias_n, row0, cols):
    _, nb, past, _ = cache_ka.shape
    t = bias_c.shape[1]
    r0 = row0 // t
    topk = min(TOPK_MAX, (past + t) // 4)
    w128 = lambda key: (lambda b, l: (r0 + b, cols[key]))
    cache = lambda n: pl.BlockSpec((None, None, past, n), lambda b, l: (l[0], b, 0, 0))
    return pl.pallas_call(
        functools.partial(_dsa_sample_body, topk),
        grid_spec=pltpu.PrefetchScalarGridSpec(
            num_scalar_prefetch=1, grid=(nb,),
            in_specs=[
                pl.BlockSpec((t, N_HEADS * HEAD_DIM), w128("qa")),
                pl.BlockSpec((t, N_IDX_HEADS * IDX_DIM), w128("qi")),
                pl.BlockSpec((t, LANES), w128("wi")),
                pl.BlockSpec((t, HEAD_DIM), w128("ka")),
                pl.BlockSpec((t, HEAD_DIM), w128("va")),
                pl.BlockSpec((t, LANES), w128("kx")),
                cache(HEAD_DIM), cache(HEAD_DIM), cache(LANES),
                pl.BlockSpec((N_HEADS, t, past), lambda b, l: (0, 0, 0)),
                pl.BlockSpec((N_HEADS, t, t), lambda b, l: (0, 0, 0)),
            ],
            out_specs=pl.BlockSpec((t, N_HEADS * HEAD_DIM), lambda b, l: (b, 0)),
            scratch_shapes=[pltpu.VMEM((t, past), F32), pltpu.VMEM((t, t), F32)]),
        out_shape=jax.ShapeDtypeStruct((nb * t, N_HEADS * HEAD_DIM), BF16),
        compiler_params=_cparams("arbitrary"),
        name="dsa_sample",
    )(lidx, projq, projq, pkv_f, pkv_b, pkv_b, pkv_b, cache_ka, cache_va, cache_kidx, bias_c, bias_n)


def _diff_sample_body(l_ref, laminit_ref, qb_ref, kn_ref, vn_ref, kc_ref, vc_ref, bc_ref, bn_ref,
                      lq1_ref, lk1_ref, lq2_ref, lk2_ref, g_ref, o_ref):
    lam_init = laminit_ref[l_ref[0]]
    lam = _lambda(lq1_ref[...], lk1_ref[...], lq2_ref[...], lk2_ref[...], lam_init)
    scale = SUB_DIM ** -0.5
    g = g_ref[...]

    def head(h, carry):
        off = pl.multiple_of(h * HEAD_DIM, HEAD_DIM)
        q1, q2 = _lane_half_masks(qb_ref[:, pl.ds(off, HEAD_DIM)])
        k_c = kc_ref[:, pl.ds(off, HEAD_DIM)].astype(BF16)
        v_c = vc_ref[:, pl.ds(off, HEAD_DIM)].astype(BF16)
        k_n = kn_ref[:, pl.ds(off, HEAD_DIM)]
        v_n = vn_ref[:, pl.ds(off, HEAD_DIM)]
        bc = bc_ref[h]
        bn = bn_ref[h]
        (e1c, e1n), r1 = _softmax_parts([_dot_nt(q1, k_c) * scale + bc, _dot_nt(q1, k_n) * scale + bn])
        (e2c, e2n), r2 = _softmax_parts([_dot_nt(q2, k_c) * scale + bc, _dot_nt(q2, k_n) * scale + bn])
        r2 = r2 * lam
        a_c = (e1c * r1 - e2c * r2).astype(BF16)
        a_n = (e1n * r1 - e2n * r2).astype(BF16)
        o = _dot(a_c, v_c) + _dot(a_n, v_n)
        o_ref[:, pl.ds(off, HEAD_DIM)] = _sub_ln(o, g, lam_init).astype(o_ref.dtype)
        return carry

    lax.fori_loop(0, N_HEADS, head, 0)


def _diff_sample(lidx, projq, pkv_b, cache_kb, cache_vb, bias_c, bias_n, lam_inits, lq1, lk1, lq2, lk2, subg,
                 row0, cols):
    _, nb, past, width = cache_kb.shape
    t = bias_c.shape[1]
    r0 = row0 // t
    vec = lambda n: pl.BlockSpec((None, 1, n), lambda b, l: (l[0], 0, 0))
    cache = pl.BlockSpec((None, None, past, width), lambda b, l: (l[0], b, 0, 0))
    return pl.pallas_call(
        _diff_sample_body,
        grid_spec=pltpu.PrefetchScalarGridSpec(
            num_scalar_prefetch=1, grid=(nb,),
            in_specs=[
                pl.BlockSpec(memory_space=pltpu.SMEM),
                pl.BlockSpec((t, width), lambda b, l: (r0 + b, cols["qb"])),
                pl.BlockSpec((t, width), lambda b, l: (r0 + b, cols["kb"])),
                pl.BlockSpec((t, width), lambda b, l: (r0 + b, cols["vb"])),
                cache, cache,
                pl.BlockSpec((N_HEADS, t, past), lambda b, l: (0, 0, 0)),
                pl.BlockSpec((N_HEADS, t, t), lambda b, l: (0, 0, 0)),
                vec(SUB_DIM), vec(SUB_DIM), vec(SUB_DIM), vec(SUB_DIM), vec(HEAD_DIM),
            ],
            out_specs=pl.BlockSpec((t, width), lambda b, l: (b, 0))),
        out_shape=jax.ShapeDtypeStruct((nb * t, width), BF16),
        compiler_params=_cparams("arbitrary"),
        name="diff_sample",
    )(lidx, lam_inits, projq, pkv_b, pkv_b, cache_kb, cache_vb, bias_c, bias_n, lq1, lk1, lq2, lk2, subg)


def _layer_norm_rows(z, g, b):
    mu = jnp.mean(z, axis=-1, keepdims=True)
    zc = z - mu
    var = jnp.mean(zc * zc, axis=-1, keepdims=True)
    return zc * lax.rsqrt(var + LN_EPS) * g + b


def _route(rl):
    lane = lax.broadcasted_iota(jnp.int32, rl.shape, 1).astype(F32)
    big = 1e6
    gl = jnp.where(lane < N_GROUPS, rl, NEG)
    gmax = gl.max(axis=-1, keepdims=True)
    gsum = jnp.sum(jnp.exp(gl - gmax), axis=-1, keepdims=True)
    g_val = 1.0 / gsum
    g_idx = jnp.min(jnp.where(gl == gmax, lane, big), axis=-1, keepdims=True)
    lo = N_GROUPS + g_idx * EXPERTS_PER_GROUP
    el = jnp.where((lane >= lo) & (lane < lo + EXPERTS_PER_GROUP), rl, NEG)
    m1 = el.max(axis=-1, keepdims=True)
    i1 = jnp.min(jnp.where(el == m1, lane, big), axis=-1, keepdims=True)
    el2 = jnp.where(lane == i1, NEG, el)
    m2 = el2.max(axis=-1, keepdims=True)
    i2 = jnp.min(jnp.where(el2 == m2, lane, big), axis=-1, keepdims=True)
    e2 = jnp.exp(m2 - m1)
    den = 1.0 / (1.0 + e2)
    w1 = den * g_val
    w2 = e2 * den * g_val
    out = jnp.where(lane == 0, i1 - N_GROUPS, 0.0)
    out = jnp.where(lane == 1, i2 - N_GROUPS, out)
    out = jnp.where(lane == 2, w1, out)
    out = jnp.where(lane == 3, w2, out)
    return out


def _out_ln_body(alpha, tn, rows, l_ref, oa_ref, ob_ref, wa_ref, wb_ref, x_ref, g_ref, b_ref, wr_ref,
                 h_ref, hb_ref, rt_ref):
    j = pl.program_id(1)
    acc = _dot(oa_ref[...], wa_ref[...]) + _dot(ob_ref[...], wb_ref[...])
    off = pl.multiple_of(j * tn, tn)
    h_ref[:, pl.ds(off, tn)] = alpha * x_ref[...] + acc

    @pl.when(j == pl.num_programs(1) - 1)
    def _():
        g = g_ref[...]
        b = b_ref[...]

        def chunk(c, carry):
            r = pl.multiple_of(c * rows, rows)
            hn = _layer_norm_rows(h_ref[pl.ds(r, rows), :], g, b)
            h_ref[pl.ds(r, rows), :] = hn
            hb_ref[pl.ds(r, rows), :] = hn.astype(BF16)
            rl = jnp.dot(hn, wr_ref[...], precision=lax.Precision.HIGHEST, preferred_element_type=F32)
            rt_ref[pl.ds(r, rows), :] = _route(rl)
            return carry

        lax.fori_loop(0, h_ref.shape[0] // rows, chunk, 0)


def _out_ln(lidx, oa, ob, w_out, x, g, b, wr, alpha, tm, tn):
    m, half = oa.shape
    d = w_out.shape[-1]
    rows = 64 if tm % 64 == 0 else tm
    return pl.pallas_call(
        functools.partial(_out_ln_body, alpha, tn, rows),
        grid_spec=pltpu.PrefetchScalarGridSpec(
            num_scalar_prefetch=1, grid=(m // tm, d // tn),
            in_specs=[
                pl.BlockSpec((tm, half), lambda i, j, l: (i, 0)),
                pl.BlockSpec((tm, half), lambda i, j, l: (i, 0)),
                pl.BlockSpec((None, half, tn), lambda i, j, l: (l[0], 0, j)),
                pl.BlockSpec((None, half, tn), lambda i, j, l: (l[0], 1, j)),
                pl.BlockSpec((tm, tn), lambda i, j, l: (i, j)),
                pl.BlockSpec((None, 1, d), lambda i, j, l: (l[0], 0, 0)),
                pl.BlockSpec((None, 1, d), lambda i, j, l: (l[0], 0, 0)),
                pl.BlockSpec((None, d, LANES), lambda i, j, l: (l[0], 0, 0)),
            ],
            out_specs=[pl.BlockSpec((tm, d), lambda i, j, l: (i, 0)),
                       pl.BlockSpec((tm, d), lambda i, j, l: (i, 0)),
                       pl.BlockSpec((tm, LANES), lambda i, j, l: (i, 0))]),
        out_shape=[jax.ShapeDtypeStruct((m, d), F32), jax.ShapeDtypeStruct((m, d), BF16),
                   jax.ShapeDtypeStruct((m, LANES), F32)],
        compiler_params=_cparams("parallel", "arbitrary"),
        name="out_proj_ln_route",
    )(lidx, oa, ob, w_out, w_out, x, g, b, wr)


def _moe_body(meta_ref, te_ref, xs_ref, gate_ref, w1_ref, w3_ref, w2_ref, o_ref):
    t = pl.program_id(1)

    @pl.when(t < meta_ref[1])
    def _():
        x = xs_ref[...]
        u = _dot(x, w1_ref[...])
        v = _dot(x, w3_ref[...])
        a = (u * (1.0 / (1.0 + jnp.exp(-u))) * v).astype(BF16)
        y = _dot(a, w2_ref[...])
        o_ref[...] = (y * gate_ref[:, 0:1]).astype(o_ref.dtype)

    @pl.when(t >= meta_ref[1])
    def _():
        o_ref[...] = jnp.zeros_like(o_ref)


def _moe(meta, tile_expert, xs, gate, w1, w3, w2, tm, nf):
    p, d = xs.shape
    f = w1.shape[-1]
    fh = f // nf
    nt = p // tm
    return pl.pallas_call(
        _moe_body,
        grid_spec=pltpu.PrefetchScalarGridSpec(
            num_scalar_prefetch=2, grid=(nf, nt),
            in_specs=[
                pl.BlockSpec((tm, d), lambda c, t, meta, te: (t, 0)),
                pl.BlockSpec((tm, LANES), lambda c, t, meta, te: (t, 0)),
                pl.BlockSpec((None, None, d, fh), lambda c, t, meta, te: (meta[0], te[t], 0, c)),
                pl.BlockSpec((None, None, d, fh), lambda c, t, meta, te: (meta[0], te[t], 0, c)),
                pl.BlockSpec((None, None, fh, d), lambda c, t, meta, te: (meta[0], te[t], c, 0)),
            ],
            out_specs=pl.BlockSpec((None, tm, d), lambda c, t, meta, te: (c, t, 0))),
        out_shape=jax.ShapeDtypeStruct((nf, p, d), BF16),
        compiler_params=_cparams("arbitrary", "arbitrary"),
        name="moe_experts",
    )(meta, tile_expert, xs, gate, w1, w3, w2)


def _dispatch_plan(route, tm):
    m = route.shape[0]
    npairs = 2 * m
    nt = npairs // tm + N_EXPERTS
    p = nt * tm
    e = route[:, 0:2].astype(jnp.int32).reshape(npairs)
    wgt = route[:, 2:4].reshape(npairs)
    order = jnp.argsort(e, stable=True).astype(jnp.int32)
    e_sorted = e[order]
    counts = jnp.zeros((N_EXPERTS,), jnp.int32).at[e].add(1)
    padded = ((counts + tm - 1) // tm) * tm
    pad_end = jnp.cumsum(padded)
    pad_start = pad_end - padded
    cnt_start = jnp.cumsum(counts) - counts
    slot_sorted = pad_start[e_sorted] + (jnp.arange(npairs, dtype=jnp.int32) - cnt_start[e_sorted])
    src = jnp.zeros((p,), jnp.int32).at[slot_sorted].set(order // 2)
    gate = jnp.zeros((p,), F32).at[slot_sorted].set(wgt[order])
    slot_of_pair = jnp.zeros((npairs,), jnp.int32).at[order].set(slot_sorted)
    tile_start = jnp.arange(nt, dtype=jnp.int32) * tm
    tile_expert = jnp.minimum(jnp.searchsorted(pad_end, tile_start, side="right"), N_EXPERTS - 1).astype(jnp.int32)
    n_used = (pad_end[-1] // tm).astype(jnp.int32)
    return src, gate, slot_of_pair.reshape(m, 2), tile_expert, n_used


def _ln2_body(alpha, l_ref, h_ref, y_ref, g_ref, b_ref, o_ref, ob_ref):
    z = alpha * h_ref[...]
    for k in range(y_ref.shape[0]):
        z = z + y_ref[k].astype(F32)
    hn = _layer_norm_rows(z, g_ref[...], b_ref[...])
    o_ref[...] = hn
    ob_ref[...] = hn.astype(BF16)


def _ln2(lidx, h, ys, g, b, alpha, tm):
    m, d = h.shape
    ny = ys.shape[0]
    return pl.pallas_call(
        functools.partial(_ln2_body, alpha),
        grid_spec=pltpu.PrefetchScalarGridSpec(
            num_scalar_prefetch=1, grid=(m // tm,),
            in_specs=[
                pl.BlockSpec((tm, d), lambda i, l: (i, 0)),
                pl.BlockSpec((ny, tm, d), lambda i, l: (0, i, 0)),
                pl.BlockSpec((None, 1, d), lambda i, l: (l[0], 0, 0)),
                pl.BlockSpec((None, 1, d), lambda i, l: (l[0], 0, 0)),
            ],
            out_specs=[pl.BlockSpec((tm, d), lambda i, l: (i, 0)),
                       pl.BlockSpec((tm, d), lambda i, l: (i, 0))]),
        out_shape=[jax.ShapeDtypeStruct((m, d), F32), jax.ShapeDtypeStruct((m, d), BF16)],
        compiler_params=_cparams("parallel"),
        name="moe_combine_ln",
    )(lidx, h, ys, g, b)


def _pick_tile(m, candidates):
    for c in candidates:
        if m % c == 0:
            return c
    raise ValueError(f"no row tile for {m} rows")


def kernel(x_prompt, x_sample, cache_ka, cache_va, cache_kidx, cache_kb, cache_vb, rel_bias, w_in, w_out,
           lambda_q1, lambda_k1, lambda_q2, lambda_k2, subln_g, ln1_g, ln1_b, ln2_g, ln2_b,
           w_route_group, w_route_expert, w1, w3, w2):
    batch, seq, d = x_prompt.shape
    dec_batch, dec_seq, _ = x_sample.shape
    depth = w_in.shape[0]
    past = cache_ka.shape[2]
    mp = batch * seq
    ms = dec_batch * dec_seq
    m = mp + ms
    alpha = (2.0 * depth) ** 0.25
    width = N_HEADS * HEAD_DIM
    assert seq % QB == 0 and seq >= 2 * WIN and mp % dec_seq == 0 and width == 2048

    in_widths = (width, HEAD_DIM, HEAD_DIM, N_IDX_HEADS * IDX_DIM, IDX_DIM, N_IDX_HEADS, width, width, width)
    splits = [int(s) for s in np.cumsum(in_widths)[:-1]]
    wqa, wka, wva, wqi, wki, wwi, wqb, wkb, wvb = jnp.split(w_in, splits, axis=-1)
    w_q = jnp.concatenate([wqa, wqb, wqi], axis=-1).astype(BF16)
    pad = jnp.zeros(wwi.shape[:-1] + (LANES - N_IDX_HEADS,), w_in.dtype)
    w_kv = jnp.concatenate([wkb, wvb, wka, wva, wki, wki, wwi, pad], axis=-1).astype(BF16)
    cols_q = {"qa": 0, "qb": 1, "qi": 4}
    cols_kv = {"kb": 0, "vb": 1, "ka": 32, "va": 33, "kx": 34, "wi": 35}
    cols = {**cols_q, **cols_kv}
    w_out_b = w_out.astype(BF16)
    w1_b = w1.astype(BF16)
    w3_b = w3.astype(BF16)
    w2_b = w2.astype(BF16)
    w_r = jnp.concatenate([w_route_group, w_route_expert,
                           jnp.zeros((depth, d, LANES - N_GROUPS - N_EXPERTS), F32)], axis=-1)
    vec3 = lambda a: a.reshape(depth, 1, a.shape[-1])
    lq1, lk1, lq2, lk2, subg = map(vec3, (lambda_q1, lambda_k1, lambda_q2, lambda_k2, subln_g))
    g1, b1, g2, b2 = map(vec3, (ln1_g, ln1_b, ln2_g, ln2_b))
    lam_inits = jnp.asarray([0.8 - 0.6 * math.exp(-0.3 * l) for l in range(depth)], F32)

    dwin, cfar = _prompt_bias_tiles(rel_bias)
    sbias = _sample_bias(rel_bias, past, dec_seq)
    dwin_a, dwin_b = dwin[:N_HEADS], dwin[N_HEADS:]
    cfar_a, cfar_b = cfar[:N_HEADS], cfar[N_HEADS:]
    sb_a_c, sb_a_n = sbias[:N_HEADS, :, :past], sbias[:N_HEADS, :, past:]
    sb_b_c, sb_b_n = sbias[N_HEADS:, :, :past], sbias[N_HEADS:, :, past:]
    cache_kb2 = cache_kb.reshape(cache_kb.shape[:3] + (width,))
    cache_vb2 = cache_vb.reshape(cache_vb.shape[:3] + (width,))
    cache_kx2 = jnp.concatenate([cache_kidx, cache_kidx], axis=-1)

    tm_proj = _pick_tile(m, (768, 512, 256, 128, 64, 32))
    tm_out = _pick_tile(m, (384, 256, 128, 64, 32))
    tm_ln2 = _pick_tile(m, (256, 128, 64, 32))
    tm_moe = 256 if m >= 1024 else 32
    nf = 2

    x0 = jnp.concatenate([x_prompt.reshape(mp, d), x_sample.reshape(ms, d)], axis=0)

    def layer(carry, l):
        x, xb = carry
        lidx = jnp.reshape(l, (1,)).astype(jnp.int32)
        projq = _proj(lidx, xb, w_q, tm_proj, 512, False)[0]
        pkv_b, pkv_f = _proj(lidx, xb, w_kv, tm_proj, 512, True)
        oa_p = _dsa_prompt(projq, pkv_b, pkv_f, dwin_a, cfar_a, batch, seq, cols)
        ob_p = _diff_prompt(lidx, projq, pkv_b, dwin_b, cfar_b, lam_inits, lq1, lk1, lq2, lk2, subg,
                            batch, seq, cols)
        oa_s = _dsa_sample(lidx, projq, pkv_b, pkv_f, cache_ka, cache_va, cache_kx2, sb_a_c, sb_a_n, mp, cols)
        ob_s = _diff_sample(lidx, projq, pkv_b, cache_kb2, cache_vb2, sb_b_c, sb_b_n, lam_inits,
                            lq1, lk1, lq2, lk2, subg, mp, cols)
        oa = jnp.concatenate([oa_p, oa_s], axis=0)
        ob = jnp.concatenate([ob_p, ob_s], axis=0)
        h1, h1b, route = _out_ln(lidx, oa, ob, w_out_b, x, g1, b1, w_r, alpha, tm_out, 512)
        src, gate, slot_of_pair, tile_expert, n_used = _dispatch_plan(route, tm_moe)
        xs = jnp.take(h1b, src, axis=0)
        gate128 = jnp.broadcast_to(gate[:, None], (gate.shape[0], LANES))
        meta = jnp.stack([lidx[0], n_used])
        parts = _moe(meta, tile_expert, xs, gate128, w1_b, w3_b, w2_b, tm_moe, nf)
        ys = jnp.concatenate([jnp.take(parts, slot_of_pair[:, 0], axis=1),
                              jnp.take(parts, slot_of_pair[:, 1], axis=1)], axis=0)
        h2, h2b = _ln2(lidx, h1, ys, g2, b2, alpha, tm_ln2)
        rows = (pkv_f[:, 2 * width:2 * width + HEAD_DIM],
                pkv_f[:, 2 * width + HEAD_DIM:2 * width + 2 * HEAD_DIM],
                pkv_f[:, 2 * width + 2 * HEAD_DIM:2 * width + 2 * HEAD_DIM + IDX_DIM],
                pkv_f[:, 0:width],
                pkv_f[:, width:2 * width])
        return (h2, h2b), rows

    (y, _), rows = lax.scan(layer, (x0, x0.astype(BF16)), jnp.arange(depth, dtype=jnp.int32))
    ka, va, kx, kb, vb = rows
    heads = (N_HEADS, 2 * SUB_DIM)

    def split(a, tail):
        return (a[:, :mp].reshape((depth, batch, seq) + tail),
                a[:, mp:].reshape((depth, dec_batch, dec_seq) + tail))

    p_ka, s_ka = split(ka, (HEAD_DIM,))
    p_va, s_va = split(va, (HEAD_DIM,))
    p_kx, s_kx = split(kx, (IDX_DIM,))
    p_kb, s_kb = split(kb, heads)
    p_vb, s_vb = split(vb, heads)
    y_prompt = y[:mp].reshape(batch, seq, d)
    y_sample = y[mp:].reshape(dec_batch, dec_seq, d)
    return (y_prompt, y_sample, p_ka, p_va, p_kx, p_kb, p_vb, s_ka, s_va, s_kx, s_kb, s_vb)
```

```python
import functools
import math

import numpy as np
import jax
import jax.numpy as jnp
from jax import lax
from jax.experimental import pallas as pl
from jax.experimental.pallas import tpu as pltpu

CHUNK = 64
HEAD_DIM = 128
N_HEADS = 16
N_IDX_HEADS = 16
IDX_DIM = 64
SUB_DIM = 64
TOPK_MAX = 256
NUM_BUCKETS = 32
MAX_DISTANCE = 128
N_GROUPS = 4
EXPERTS_PER_GROUP = 4
N_EXPERTS = 16
LN_EPS = 1e-5
RMS_EPS = 1e-5

LANES = 128
QB = 128
WIN = 2 * QB
FAR_STEP = 512
NEG = -1e30
INT_MIN = -2147483648
LOG2E = 1.4426950408889634
VMEM_LIMIT = 56 * 1024 * 1024

F32 = jnp.float32
BF16 = jnp.bfloat16


def _cparams(*sem):
    return pltpu.CompilerParams(dimension_semantics=sem, vmem_limit_bytes=VMEM_LIMIT)


def _dot_nt(a, b):
    return lax.dot_general(a, b, (((1,), (1,)), ((), ())), preferred_element_type=F32)


def _dot(a, b):
    return jnp.dot(a, b, preferred_element_type=F32)


def _bucket_np(rel):
    nb = NUM_BUCKETS // 2
    ret = np.where(rel > 0, nb, 0)
    n = np.abs(rel)
    max_exact = nb // 2
    nf = np.maximum(n, 1).astype(np.float32)
    frac = np.log(nf / np.float32(max_exact)) / np.float32(math.log(MAX_DISTANCE / max_exact))
    large = max_exact + (frac.astype(np.float32) * np.float32(nb - max_exact)).astype(np.int32)
    large = np.minimum(large, nb - 1)
    return (ret + np.where(n < max_exact, n, large)).astype(np.int32)


def _prompt_bias_tiles(rel_bias):
    ql = np.arange(QB)[:, None]
    c = np.arange(WIN)[None, :]
    rel = c - QB - ql
    bucket = _bucket_np(rel)
    kl = c - QB
    adm = (c < QB) | ((kl // CHUNK) <= (ql // CHUNK))
    tiles = jnp.transpose(rel_bias[bucket], (2, 0, 1)) * LOG2E
    tiles = jnp.where(jnp.asarray(adm)[None], tiles, NEG)
    far = rel_bias[int(_bucket_np(np.array([-(QB + 1)]))[0])] * LOG2E
    return tiles.astype(F32), far.astype(F32)


def _sample_bias(rel_bias, past_len, dec_seq):
    q = past_len + np.arange(dec_seq)[:, None]
    k = np.arange(past_len + dec_seq)[None, :]
    return (jnp.transpose(rel_bias[_bucket_np(k - q)], (2, 0, 1)) * LOG2E).astype(F32)


def _proj_body(l_ref, x_ref, w_ref, *rest):
    acc = _dot(x_ref[...], w_ref[...])
    if len(rest) == 1:
        rest[0][...] = acc.astype(BF16)
    else:
        _, ob_ref, of_ref = rest
        ob_ref[...] = acc.astype(BF16)
        of_ref[...] = acc


def _proj(lidx, xb, w, tm, tn, rows_all=None):
    m, d = xb.shape
    n = w.shape[-1]
    in_specs = [pl.BlockSpec((tm, d), lambda i, j, l: (i, 0)),
                pl.BlockSpec((None, d, tn), lambda i, j, l: (l[0], 0, j))]
    out_shape = [jax.ShapeDtypeStruct((m, n), BF16)]
    out_specs = [pl.BlockSpec((tm, tn), lambda i, j, l: (i, j))]
    args = [lidx, xb, w]
    aliases = {}
    if rows_all is not None:
        in_specs.append(pl.BlockSpec(memory_space=pl.ANY))
        out_shape.append(jax.ShapeDtypeStruct(rows_all.shape, F32))
        out_specs.append(pl.BlockSpec((None, tm, tn), lambda i, j, l: (l[0], i, j)))
        args.append(rows_all)
        aliases = {3: 1}
    return pl.pallas_call(
        _proj_body,
        grid_spec=pltpu.PrefetchScalarGridSpec(
            num_scalar_prefetch=1, grid=(m // tm, n // tn), in_specs=in_specs, out_specs=out_specs),
        out_shape=out_shape,
        input_output_aliases=aliases,
        compiler_params=_cparams("parallel", "arbitrary"),
        name="in_proj" if rows_all is None else "in_proj_rows",
    )(*args)


def _sortable(score):
    bits = pltpu.bitcast(score + 0.0, jnp.int32)
    return jnp.where(bits < 0, bits ^ jnp.int32(0x7FFFFFFF), bits)


def _kth_largest(keys, k):
    t = keys[0].shape[0]

    def step(it, thr):
        bit = jnp.left_shift(jnp.int32(1), 31 - it)
        cand = thr + bit
        cnt = jnp.zeros((t, 1), F32)
        for kk in keys:
            cnt = cnt + jnp.sum(jnp.where(kk >= cand, 1.0, 0.0), axis=-1, keepdims=True)
        return jnp.where(cnt >= k, cand, thr)

    return lax.fori_loop(0, 32, step, jnp.full((t, 1), INT_MIN, jnp.int32))


def _index_scores(qi, wi, kparts):
    t = qi.shape[0]
    lane = lax.broadcasted_iota(jnp.int32, (t, LANES), 1)
    w = wi * (N_IDX_HEADS ** -0.5 * IDX_DIM ** -0.5)
    acc = [jnp.zeros((t, kp.shape[0]), F32) for kp in kparts]
    for p in range(N_IDX_HEADS // 2):
        qp = qi[:, p * LANES:(p + 1) * LANES]
        halves = (jnp.where(lane < IDX_DIM, qp, jnp.zeros_like(qp)),
                  jnp.where(lane >= IDX_DIM, qp, jnp.zeros_like(qp)))
        for s, qh in enumerate(halves):
            h = 2 * p + s
            wh = w[:, h:h + 1]
            for j, kp in enumerate(kparts):
                acc[j] = acc[j] + jnp.maximum(_dot_nt(qh, kp), 0.0) * wh
    return acc


def _softmax2(parts, shifts):
    m = None
    for tp, c in zip(parts, shifts):
        mj = tp.max(axis=-1, keepdims=True)
        if c is not None:
            mj = mj + c
        m = mj if m is None else jnp.maximum(m, mj)
    es, l = [], None
    for tp, c in zip(parts, shifts):
        e = jnp.exp2(tp - (m if c is None else m - c))
        es.append(e)
        lj = e.sum(axis=-1, keepdims=True)
        l = lj if l is None else l + lj
    return es, 1.0 / l


def _lane_half_masks(q):
    lane = lax.broadcasted_iota(jnp.int32, q.shape, 1)
    z = jnp.zeros_like(q)
    return jnp.where(lane < SUB_DIM, q, z), jnp.where(lane >= SUB_DIM, q, z)


def _lambda(lq1, lk1, lq2, lk2, lam_init):
    return (jnp.exp(jnp.sum(lq1 * lk1, axis=-1, keepdims=True))
            - jnp.exp(jnp.sum(lq2 * lk2, axis=-1, keepdims=True)) + lam_init)


def _sub_ln(o, g, lam_init):
    return o * lax.rsqrt(jnp.mean(o * o, axis=-1, keepdims=True) + RMS_EPS) * g * (1.0 - lam_init)


def _far_buckets(nq, far_max):
    lens = list(range(0, far_max, FAR_STEP)) + [far_max]
    if len(lens) > 2 and lens[-1] - lens[-2] < FAR_STEP:
        del lens[-2]
    out, lo = [], 0
    for fl in lens:
        hi = min(nq, fl // QB + 2)
        if hi > lo:
            out.append((lo, hi, fl))
            lo = hi
    assert lo == nq
    return out


def _window(ref, prev0, diag0, cols=slice(None)):
    return jnp.concatenate([ref[pl.ds(prev0, QB), cols], ref[pl.ds(diag0, QB), cols]], axis=0)


def _dsa_block(fl, topk, i, cfar_ref, qa_ref, qi_ref, wi_ref, k_ref, v_ref, kx_ref, dwin_ref,
               o_ref, mfar_ref, mwin_ref):
    prev0 = pl.multiple_of(jnp.maximum(i - 1, 0) * QB, QB)
    diag0 = pl.multiple_of(i * QB, QB)

    kparts = ([kx_ref[0:fl, :]] if fl else []) + [_window(kx_ref, prev0, diag0)]
    scores = _index_scores(qi_ref[...], wi_ref[...], kparts)
    s_win = scores[-1]
    col_win = lax.broadcasted_iota(jnp.int32, s_win.shape, 1)
    row_win = lax.broadcasted_iota(jnp.int32, s_win.shape, 0)
    ok_win = ((col_win < QB + CHUNK) | (row_win >= CHUNK)) & ((col_win >= QB) | (i > 0))
    key_win = jnp.where(ok_win, _sortable(s_win), INT_MIN)
    keys = [key_win]
    if fl:
        s_far = scores[0]
        col_far = lax.broadcasted_iota(jnp.int32, s_far.shape, 1)
        ok_far = col_far < (i - 1) * QB
        key_far = jnp.where(ok_far, _sortable(s_far), INT_MIN)
        keys = [key_far, key_win]
    thr = _kth_largest(keys, topk)
    mwin_ref[...] = jnp.where(ok_win & (key_win >= thr), 0.0, NEG)
    if fl:
        mfar_ref[:, 0:fl] = jnp.where(ok_far & (key_far >= thr), 0.0, NEG)

    k_win = _window(k_ref, prev0, diag0)
    v_win = _window(v_ref, prev0, diag0)
    c1 = HEAD_DIM ** -0.5 * LOG2E

    def head(h, carry):
        off = pl.multiple_of(h * HEAD_DIM, HEAD_DIM)
        q = qa_ref[:, pl.ds(off, HEAD_DIM)]
        tw = _dot_nt(q, k_win) * c1 + (dwin_ref[h] + mwin_ref[...])
        if fl:
            tf = _dot_nt(q, k_ref[0:fl, :]) * c1 + mfar_ref[:, 0:fl]
            (ef, ew), rl = _softmax2([tf, tw], [cfar_ref[h], None])
            o = (_dot(ef.astype(BF16), v_ref[0:fl, :]) + _dot(ew.astype(BF16), v_win)) * rl
        else:
            (ew,), rl = _softmax2([tw], [None])
            o = _dot(ew.astype(BF16), v_win) * rl
        o_ref[:, pl.ds(off, HEAD_DIM)] = o.astype(o_ref.dtype)
        return carry

    lax.fori_loop(0, N_HEADS, head, 0, unroll=2)


def _dsa_prompt_body(buckets, topk, l_ref, *refs):
    i = pl.program_id(1)
    for lo, hi, fl in buckets:
        pl.when((i >= lo) & (i < hi))(functools.partial(_dsa_block, fl, topk, i, *refs))


def _dsa_prompt(lidx, projq, pkv_b, rows_all, dwin, cfar, batch, seq, m_total, cols):
    nq = seq // QB
    far_max = seq - WIN
    topk = min(TOPK_MAX, seq // 4)
    width = N_HEADS * HEAD_DIM
    return pl.pallas_call(
        functools.partial(_dsa_prompt_body, _far_buckets(nq, far_max), topk),
        grid_spec=pltpu.PrefetchScalarGridSpec(
            num_scalar_prefetch=1, grid=(batch, nq),
            in_specs=[
                pl.BlockSpec(memory_space=pltpu.SMEM),
                pl.BlockSpec((QB, width), lambda b, i, l: (b * nq + i, cols["qa"])),
                pl.BlockSpec((QB, N_IDX_HEADS * IDX_DIM), lambda b, i, l: (b * nq + i, cols["qi"])),
                pl.BlockSpec((None, QB, LANES), lambda b, i, l: (l[0], b * nq + i, cols["wi"])),
                pl.BlockSpec((seq, HEAD_DIM), lambda b, i, l: (b, cols["ka"])),
                pl.BlockSpec((seq, HEAD_DIM), lambda b, i, l: (b, cols["va"])),
                pl.BlockSpec((seq, LANES), lambda b, i, l: (b, cols["kx"])),
                pl.BlockSpec((N_HEADS, QB, WIN), lambda b, i, l: (0, 0, 0)),
            ],
            out_specs=pl.BlockSpec((QB, width), lambda b, i, l: (b * nq + i, 0)),
            scratch_shapes=[pltpu.VMEM((QB, far_max), F32), pltpu.VMEM((QB, WIN), F32)]),
        out_shape=jax.ShapeDtypeStruct((m_total, width), BF16),
        compiler_params=_cparams("parallel", "arbitrary"),
        name="dsa_prompt",
    )(lidx, cfar, projq, projq, rows_all, pkv_b, pkv_b, pkv_b, dwin)


def _diff_block(fl, i, l_ref, cfar_ref, laminit_ref, qb_ref, kb_ref, vb_ref, dwin_ref,
                lq1_ref, lk1_ref, lq2_ref, lk2_ref, g_ref, o_ref, mfar_ref, mwin_ref):
    prev0 = pl.multiple_of(jnp.maximum(i - 1, 0) * QB, QB)
    diag0 = pl.multiple_of(i * QB, QB)
    lam_init = laminit_ref[l_ref[0]]
    lam = _lambda(lq1_ref[...], lk1_ref[...], lq2_ref[...], lk2_ref[...], lam_init)
    col_win = lax.broadcasted_iota(jnp.int32, (QB, WIN), 1)
    mwin_ref[...] = jnp.where((col_win >= QB) | (i > 0), 0.0, NEG)
    if fl:
        col_far = lax.broadcasted_iota(jnp.int32, (QB, fl), 1)
        mfar_ref[:, 0:fl] = jnp.where(col_far < (i - 1) * QB, 0.0, NEG)
    c1 = SUB_DIM ** -0.5 * LOG2E
    g = g_ref[...]

    def head(h, carry):
        off = pl.multiple_of(h * HEAD_DIM, HEAD_DIM)
        hc = pl.ds(off, HEAD_DIM)
        q1, q2 = _lane_half_masks(qb_ref[:, hc])
        k_win = _window(kb_ref, prev0, diag0, hc)
        v_win = _window(vb_ref, prev0, diag0, hc)
        bias_win = dwin_ref[h] + mwin_ref[...]
        if fl:
            k_far = kb_ref[0:fl, hc]
            shifts = [cfar_ref[h], None]
            (e1f, e1w), r1 = _softmax2([_dot_nt(q1, k_far) * c1 + mfar_ref[:, 0:fl],
                                        _dot_nt(q1, k_win) * c1 + bias_win], shifts)
            (e2f, e2w), r2 = _softmax2([_dot_nt(q2, k_far) * c1 + mfar_ref[:, 0:fl],
                                        _dot_nt(q2, k_win) * c1 + bias_win], shifts)
            r2 = r2 * lam
            o = (_dot((e1f * r1 - e2f * r2).astype(BF16), vb_ref[0:fl, hc])
                 + _dot((e1w * r1 - e2w * r2).astype(BF16), v_win))
        else:
            (e1w,), r1 = _softmax2([_dot_nt(q1, k_win) * c1 + bias_win], [None])
            (e2w,), r2 = _softmax2([_dot_nt(q2, k_win) * c1 + bias_win], [None])
            r2 = r2 * lam
            o = _dot((e1w * r1 - e2w * r2).astype(BF16), v_win)
        o_ref[:, hc] = _sub_ln(o, g, lam_init).astype(o_ref.dtype)
        return carry

    lax.fori_loop(0, N_HEADS, head, 0, unroll=2)


def _diff_prompt_body(buckets, l_ref, *refs):
    i = pl.program_id(1)
    for lo, hi, fl in buckets:
        pl.when((i >= lo) & (i < hi))(functools.partial(_diff_block, fl, i, l_ref, *refs))


def _diff_prompt(lidx, projq, pkv_b, dwin, cfar, lam_inits, lq1, lk1, lq2, lk2, subg, batch, seq, m_total, cols):
    nq = seq // QB
    far_max = seq - WIN
    width = N_HEADS * HEAD_DIM
    vec = lambda n: pl.BlockSpec((None, 1, n), lambda b, i, l: (l[0], 0, 0))
    return pl.pallas_call(
        functools.partial(_diff_prompt_body, _far_buckets(nq, far_max)),
        grid_spec=pltpu.PrefetchScalarGridSpec(
            num_scalar_prefetch=1, grid=(batch, nq),
            in_specs=[
                pl.BlockSpec(memory_space=pltpu.SMEM),
                pl.BlockSpec(memory_space=pltpu.SMEM),
                pl.BlockSpec((QB, width), lambda b, i, l: (b * nq + i, cols["qb"])),
                pl.BlockSpec((seq, width), lambda b, i, l: (b, cols["kb"])),
                pl.BlockSpec((seq, width), lambda b, i, l: (b, cols["vb"])),
                pl.BlockSpec((N_HEADS, QB, WIN), lambda b, i, l: (0, 0, 0)),
                vec(SUB_DIM), vec(SUB_DIM), vec(SUB_DIM), vec(SUB_DIM), vec(HEAD_DIM),
            ],
            out_specs=pl.BlockSpec((QB, width), lambda b, i, l: (b * nq + i, 0)),
            scratch_shapes=[pltpu.VMEM((QB, far_max), F32), pltpu.VMEM((QB, WIN), F32)]),
        out_shape=jax.ShapeDtypeStruct((m_total, width), BF16),
        compiler_params=_cparams("parallel", "arbitrary"),
        name="diff_prompt",
    )(lidx, cfar, lam_inits, projq, pkv_b, pkv_b, dwin, lq1, lk1, lq2, lk2, subg)


def _dsa_sample_body(topk, l_ref, qa_ref, qi_ref, wi_ref, kn_ref, vn_ref, kxn_ref, kc_ref, vc_ref, kxc_ref,
                     bc_ref, bn_ref, oin_ref, o_ref, mc_ref, mn_ref):
    s_c, s_n = _index_scores(qi_ref[...], wi_ref[...], [kxc_ref[...].astype(BF16), kxn_ref[...]])
    key_c = _sortable(s_c)
    key_n = _sortable(s_n)
    thr = _kth_largest([key_c, key_n], topk)
    mc_ref[...] = jnp.where(key_c >= thr, 0.0, NEG)
    mn_ref[...] = jnp.where(key_n >= thr, 0.0, NEG)
    k_c = kc_ref[...].astype(BF16)
    v_c = vc_ref[...].astype(BF16)
    k_n = kn_ref[...]
    v_n = vn_ref[...]
    c1 = HEAD_DIM ** -0.5 * LOG2E

    def head(h, carry):
        off = pl.multiple_of(h * HEAD_DIM, HEAD_DIM)
        q = qa_ref[:, pl.ds(off, HEAD_DIM)]
        tc = _dot_nt(q, k_c) * c1 + (bc_ref[h] + mc_ref[...])
        tn = _dot_nt(q, k_n) * c1 + (bn_ref[h] + mn_ref[...])
        (ec, en), rl = _softmax2([tc, tn], [None, None])
        o = (_dot(ec.astype(BF16), v_c) + _dot(en.astype(BF16), v_n)) * rl
        o_ref[:, pl.ds(off, HEAD_DIM)] = o.astype(o_ref.dtype)
        return carry

    lax.fori_loop(0, N_HEADS, head, 0)


def _dsa_sample(lidx, projq, pkv_b, rows_all, cache_ka, cache_va, cache_kx2, bias_c, bias_n, o_all, row0, cols):
    _, nb, past, _ = cache_ka.shape
    t = bias_c.shape[1]
    r0 = row0 // t
    topk = min(TOPK_MAX, (past + t) // 4)
    width = N_HEADS * HEAD_DIM
    w128 = lambda key: (lambda b, l: (r0 + b, cols[key]))
    cache = lambda n: pl.BlockSpec((None, None, past, n), lambda b, l: (l[0], b, 0, 0))
    return pl.pallas_call(
        functools.partial(_dsa_sample_body, topk),
        grid_spec=pltpu.PrefetchScalarGridSpec(
            num_scalar_prefetch=1, grid=(nb,),
            in_specs=[
                pl.BlockSpec((t, width), w128("qa")),
                pl.BlockSpec((t, N_IDX_HEADS * IDX_DIM), w128("qi")),
                pl.BlockSpec((None, t, LANES), lambda b, l: (l[0], r0 + b, cols["wi"])),
                pl.BlockSpec((t, HEAD_DIM), w128("ka")),
                pl.BlockSpec((t, HEAD_DIM), w128("va")),
                pl.BlockSpec((t, LANES), w128("kx")),
                cache(HEAD_DIM), cache(HEAD_DIM), cache(LANES),
                pl.BlockSpec((N_HEADS, t, past), lambda b, l: (0, 0, 0)),
                pl.BlockSpec((N_HEADS, t, t), lambda b, l: (0, 0, 0)),
                pl.BlockSpec(memory_space=pl.ANY),
            ],
            out_specs=pl.BlockSpec((t, width), lambda b, l: (r0 + b, 0)),
            scratch_shapes=[pltpu.VMEM((t, past), F32), pltpu.VMEM((t, t), F32)]),
        out_shape=jax.ShapeDtypeStruct(o_all.shape, BF16),
        input_output_aliases={12: 0},
        compiler_params=_cparams("arbitrary"),
        name="dsa_sample",
    )(lidx, projq, projq, rows_all, pkv_b, pkv_b, pkv_b, cache_ka, cache_va, cache_kx2, bias_c, bias_n, o_all)


def _diff_sample_body(l_ref, laminit_ref, qb_ref, kn_ref, vn_ref, kc_ref, vc_ref, bc_ref, bn_ref,
                      lq1_ref, lk1_ref, lq2_ref, lk2_ref, g_ref, oin_ref, o_ref):
    lam_init = laminit_ref[l_ref[0]]
    lam = _lambda(lq1_ref[...], lk1_ref[...], lq2_ref[...], lk2_ref[...], lam_init)
    c1 = SUB_DIM ** -0.5 * LOG2E
    g = g_ref[...]

    def head(h, carry):
        off = pl.multiple_of(h * HEAD_DIM, HEAD_DIM)
        hc = pl.ds(off, HEAD_DIM)
        q1, q2 = _lane_half_masks(qb_ref[:, hc])
        k_c = kc_ref[:, hc].astype(BF16)
        v_c = vc_ref[:, hc].astype(BF16)
        k_n = kn_ref[:, hc]
        v_n = vn_ref[:, hc]
        bc = bc_ref[h]
        bn = bn_ref[h]
        none2 = [None, None]
        (e1c, e1n), r1 = _softmax2([_dot_nt(q1, k_c) * c1 + bc, _dot_nt(q1, k_n) * c1 + bn], none2)
        (e2c, e2n), r2 = _softmax2([_dot_nt(q2, k_c) * c1 + bc, _dot_nt(q2, k_n) * c1 + bn], none2)
        r2 = r2 * lam
        o = (_dot((e1c * r1 - e2c * r2).astype(BF16), v_c) + _dot((e1n * r1 - e2n * r2).astype(BF16), v_n))
        o_ref[:, hc] = _sub_ln(o, g, lam_init).astype(o_ref.dtype)
        return carry

    lax.fori_loop(0, N_HEADS, head, 0)


def _diff_sample(lidx, projq, pkv_b, cache_kb, cache_vb, bias_c, bias_n, lam_inits, lq1, lk1, lq2, lk2, subg,
                 o_all, row0, cols):
    _, nb, past, width = cache_kb.shape
    t = bias_c.shape[1]
    r0 = row0 // t
    vec = lambda n: pl.BlockSpec((None, 1, n), lambda b, l: (l[0], 0, 0))
    cache = pl.BlockSpec((None, None, past, width), lambda b, l: (l[0], b, 0, 0))
    return pl.pallas_call(
        _diff_sample_body,
        grid_spec=pltpu.PrefetchScalarGridSpec(
            num_scalar_prefetch=1, grid=(nb,),
            in_specs=[
                pl.BlockSpec(memory_space=pltpu.SMEM),
                pl.BlockSpec((t, width), lambda b, l: (r0 + b, cols["qb"])),
                pl.BlockSpec((t, width), lambda b, l: (r0 + b, cols["kb"])),
                pl.BlockSpec((t, width), lambda b, l: (r0 + b, cols["vb"])),
                cache, cache,
                pl.BlockSpec((N_HEADS, t, past), lambda b, l: (0, 0, 0)),
                pl.BlockSpec((N_HEADS, t, t), lambda b, l: (0, 0, 0)),
                vec(SUB_DIM), vec(SUB_DIM), vec(SUB_DIM), vec(SUB_DIM), vec(HEAD_DIM),
                pl.BlockSpec(memory_space=pl.ANY),
            ],
            out_specs=pl.BlockSpec((t, width), lambda b, l: (r0 + b, 0))),
        out_shape=jax.ShapeDtypeStruct(o_all.shape, BF16),
        input_output_aliases={14: 0},
        compiler_params=_cparams("arbitrary"),
        name="diff_sample",
    )(lidx, lam_inits, projq, pkv_b, pkv_b, cache_kb, cache_vb, bias_c, bias_n, lq1, lk1, lq2, lk2, subg, o_all)


def _layer_norm_rows(z, g, b):
    mu = jnp.mean(z, axis=-1, keepdims=True)
    zc = z - mu
    var = jnp.mean(zc * zc, axis=-1, keepdims=True)
    return zc * lax.rsqrt(var + LN_EPS) * g + b


def _route(rl):
    lane = lax.broadcasted_iota(jnp.int32, rl.shape, 1).astype(F32)
    big = 1e6
    gl = jnp.where(lane < N_GROUPS, rl, NEG)
    gmax = gl.max(axis=-1, keepdims=True)
    gsum = jnp.sum(jnp.exp(gl - gmax), axis=-1, keepdims=True)
    g_val = 1.0 / gsum
    g_idx = jnp.min(jnp.where(gl == gmax, lane, big), axis=-1, keepdims=True)
    lo = N_GROUPS + g_idx * EXPERTS_PER_GROUP
    el = jnp.where((lane >= lo) & (lane < lo + EXPERTS_PER_GROUP), rl, NEG)
    m1 = el.max(axis=-1, keepdims=True)
    i1 = jnp.min(jnp.where(el == m1, lane, big), axis=-1, keepdims=True)
    el2 = jnp.where(lane == i1, NEG, el)
    m2 = el2.max(axis=-1, keepdims=True)
    i2 = jnp.min(jnp.where(el2 == m2, lane, big), axis=-1, keepdims=True)
    e2 = jnp.exp(m2 - m1)
    den = 1.0 / (1.0 + e2)
    w1 = den * g_val
    w2 = e2 * den * g_val
    out = jnp.where(lane == 0, i1 - N_GROUPS, 0.0)
    out = jnp.where(lane == 1, i2 - N_GROUPS, out)
    out = jnp.where(lane == 2, w1, out)
    out = jnp.where(lane == 3, w2, out)
    return out


def _out_ln_body(alpha, tn, rows, l_ref, oa_ref, ob_ref, wa_ref, wb_ref, x_ref, g_ref, b_ref, wr_ref,
                 h_ref, rt_ref):
    j = pl.program_id(1)
    acc = _dot(oa_ref[...], wa_ref[...]) + _dot(ob_ref[...], wb_ref[...])
    off = pl.multiple_of(j * tn, tn)
    h_ref[:, pl.ds(off, tn)] = alpha * x_ref[...] + acc

    @pl.when(j == pl.num_programs(1) - 1)
    def _():
        g = g_ref[...]
        b = b_ref[...]

        def chunk(c, carry):
            r = pl.multiple_of(c * rows, rows)
            hn = _layer_norm_rows(h_ref[pl.ds(r, rows), :], g, b)
            h_ref[pl.ds(r, rows), :] = hn
            rt_ref[pl.ds(r, rows), :] = _route(_dot(hn.astype(BF16), wr_ref[...]))
            return carry

        lax.fori_loop(0, h_ref.shape[0] // rows, chunk, 0)


def _out_ln(lidx, oa, ob, w_out, x, g, b, wr, alpha, tm, tn):
    m, half = oa.shape
    d = w_out.shape[-1]
    rows = next(r for r in (64, 48, 32, 16, 8) if tm % r == 0)
    return pl.pallas_call(
        functools.partial(_out_ln_body, alpha, tn, rows),
        grid_spec=pltpu.PrefetchScalarGridSpec(
            num_scalar_prefetch=1, grid=(m // tm, d // tn),
            in_specs=[
                pl.BlockSpec((tm, half), lambda i, j, l: (i, 0)),
                pl.BlockSpec((tm, half), lambda i, j, l: (i, 0)),
                pl.BlockSpec((None, half, tn), lambda i, j, l: (l[0], 0, j)),
                pl.BlockSpec((None, half, tn), lambda i, j, l: (l[0], 1, j)),
                pl.BlockSpec((tm, tn), lambda i, j, l: (i, j)),
                pl.BlockSpec((None, 1, d), lambda i, j, l: (l[0], 0, 0)),
                pl.BlockSpec((None, 1, d), lambda i, j, l: (l[0], 0, 0)),
                pl.BlockSpec((None, d, LANES), lambda i, j, l: (l[0], 0, 0)),
            ],
            out_specs=[pl.BlockSpec((tm, d), lambda i, j, l: (i, 0)),
                       pl.BlockSpec((tm, LANES), lambda i, j, l: (i, 0))]),
        out_shape=[jax.ShapeDtypeStruct((m, d), F32), jax.ShapeDtypeStruct((m, LANES), F32)],
        compiler_params=_cparams("parallel", "arbitrary"),
        name="out_proj_ln_route",
    )(lidx, oa, ob, w_out, w_out, x, g, b, wr)


def _moe_body(tm, meta_ref, te_ref, src_ref, nxt_ref, dst_ref, gate_ref, h_hbm, w1_ref, w3_ref, w2_ref,
              y_hbm, xbuf, obuf, gsem, ssem):
    t = pl.program_id(0)
    n_used = meta_ref[1]
    slot = lax.rem(t, 2)

    def gather(idx_ref, s):
        def row(r, carry):
            pltpu.make_async_copy(h_hbm.at[pl.ds(idx_ref[0, r], 1)], xbuf.at[s, pl.ds(r, 1)], gsem.at[s]).start()
            return carry
        lax.fori_loop(0, tm, row, 0, unroll=8)

    def wait_gather(s):
        pltpu.make_async_copy(h_hbm.at[pl.ds(0, tm)], xbuf.at[s], gsem.at[s]).wait()

    def wait_scatter(s):
        pltpu.make_async_copy(obuf.at[s], y_hbm.at[pl.ds(0, tm)], ssem.at[s]).wait()

    @pl.when(t == 0)
    def _():
        gather(src_ref, 0)

    @pl.when(t + 1 < n_used)
    def _():
        gather(nxt_ref, 1 - slot)

    @pl.when(t < n_used)
    def _():
        wait_gather(slot)

        @pl.when(t >= 2)
        def _():
            wait_scatter(slot)

        x = xbuf[slot].astype(BF16)
        u = _dot(x, w1_ref[...])
        v = _dot(x, w3_ref[...])
        a = (u * (1.0 / (1.0 + jnp.exp(-u))) * v).astype(BF16)
        obuf[slot] = _dot(a, w2_ref[...]) * gate_ref[:, 0:1]

        def row(r, carry):
            pltpu.make_async_copy(obuf.at[slot, pl.ds(r, 1)], y_hbm.at[pl.ds(dst_ref[0, r], 1)], ssem.at[slot]).start()
            return carry
        lax.fori_loop(0, tm, row, 0, unroll=8)

    @pl.when(t == pl.num_programs(0) - 1)
    def _():
        wait_scatter(lax.rem(n_used - 1, 2))

        @pl.when(n_used >= 2)
        def _():
            wait_scatter(lax.rem(n_used, 2))


def _moe(meta, tile_expert, src3, dst3, gate, h, w1, w3, w2, tm, y_rows):
    m, d = h.shape
    f = w1.shape[-1]
    nt = src3.shape[0]
    idx_spec = lambda imap: pl.BlockSpec((None, 1, tm), imap, memory_space=pltpu.SMEM)
    wspec = lambda shape: pl.BlockSpec((None, None) + shape, lambda t, meta, te: (meta[0], te[t], 0, 0),
                                       pipeline_mode=pl.Buffered(1))
    return pl.pallas_call(
        functools.partial(_moe_body, tm),
        grid_spec=pltpu.PrefetchScalarGridSpec(
            num_scalar_prefetch=2, grid=(nt,),
            in_specs=[
                idx_spec(lambda t, meta, te: (t, 0, 0)),
                idx_spec(lambda t, meta, te: (jnp.minimum(t + 1, nt - 1), 0, 0)),
                idx_spec(lambda t, meta, te: (t, 0, 0)),
                pl.BlockSpec((tm, LANES), lambda t, meta, te: (t, 0)),
                pl.BlockSpec(memory_space=pl.ANY),
                wspec((d, f)), wspec((d, f)), wspec((f, d)),
            ],
            out_specs=pl.BlockSpec(memory_space=pl.ANY),
            scratch_shapes=[pltpu.VMEM((2, tm, d), F32), pltpu.VMEM((2, tm, d), F32),
                            pltpu.SemaphoreType.DMA((2,)), pltpu.SemaphoreType.DMA((2,))]),
        out_shape=jax.ShapeDtypeStruct((y_rows, d), F32),
        compiler_params=_cparams("arbitrary"),
        name="moe_experts",
    )(meta, tile_expert, src3, src3, dst3, gate, h, w1, w3, w2)


def _dispatch_plan(route, tm):
    m = route.shape[0]
    npairs = 2 * m
    nt = npairs // tm + N_EXPERTS
    p = nt * tm
    e = route[:, 0:2].astype(jnp.int32).reshape(npairs)
    wgt = route[:, 2:4].reshape(npairs)
    order = jnp.argsort(e, stable=True).astype(jnp.int32)
    counts = jnp.sum((e[:, None] == jnp.arange(N_EXPERTS, dtype=jnp.int32)[None, :]).astype(jnp.int32), axis=0)
    padded = ((counts + tm - 1) // tm) * tm
    pad_end = jnp.cumsum(padded)
    pad_start = pad_end - padded
    cnt_start = jnp.cumsum(counts) - counts
    tile_start = jnp.arange(nt, dtype=jnp.int32) * tm
    tile_expert = jnp.minimum(jnp.sum((pad_end[None, :] <= tile_start[:, None]).astype(jnp.int32), axis=1),
                              N_EXPERTS - 1)
    slot = jnp.arange(p, dtype=jnp.int32)
    se = jnp.repeat(tile_expert, tm)
    rank = slot - pad_start[se]
    valid = (rank >= 0) & (rank < counts[se])
    pair = order[jnp.clip(cnt_start[se] + rank, 0, npairs - 1)]
    src = jnp.where(valid, pair // 2, 0)
    dst = jnp.where(valid, (pair % 2) * m + pair // 2, npairs + slot)
    gate = jnp.where(valid, wgt[pair], 0.0)
    n_used = (pad_end[-1] // tm).astype(jnp.int32)
    gate128 = jnp.broadcast_to(gate[:, None], (p, LANES))
    return src.reshape(nt, 1, tm), dst.reshape(nt, 1, tm), gate128, tile_expert.astype(jnp.int32), n_used


def _ln2_body(alpha, l_ref, h_ref, y0_ref, y1_ref, g_ref, b_ref, o_ref, ob_ref):
    z = alpha * h_ref[...] + y0_ref[...] + y1_ref[...]
    hn = _layer_norm_rows(z, g_ref[...], b_ref[...])
    o_ref[...] = hn
    ob_ref[...] = hn.astype(BF16)


def _ln2(lidx, h, y, g, b, alpha, tm):
    m, d = h.shape
    nblk = m // tm
    return pl.pallas_call(
        functools.partial(_ln2_body, alpha),
        grid_spec=pltpu.PrefetchScalarGridSpec(
            num_scalar_prefetch=1, grid=(nblk,),
            in_specs=[
                pl.BlockSpec((tm, d), lambda i, l: (i, 0)),
                pl.BlockSpec((tm, d), lambda i, l: (i, 0)),
                pl.BlockSpec((tm, d), lambda i, l: (i + nblk, 0)),
                pl.BlockSpec((None, 1, d), lambda i, l: (l[0], 0, 0)),
                pl.BlockSpec((None, 1, d), lambda i, l: (l[0], 0, 0)),
            ],
            out_specs=[pl.BlockSpec((tm, d), lambda i, l: (i, 0)),
                       pl.BlockSpec((tm, d), lambda i, l: (i, 0))]),
        out_shape=[jax.ShapeDtypeStruct((m, d), F32), jax.ShapeDtypeStruct((m, d), BF16)],
        compiler_params=_cparams("parallel"),
        name="moe_combine_ln",
    )(lidx, h, y, y, g, b)


def _pick_tile(m, candidates):
    for c in candidates:
        if m % c == 0:
            return c
    raise ValueError(f"no row tile for {m} rows")


def kernel(x_prompt, x_sample, cache_ka, cache_va, cache_kidx, cache_kb, cache_vb, rel_bias, w_in, w_out,
           lambda_q1, lambda_k1, lambda_q2, lambda_k2, subln_g, ln1_g, ln1_b, ln2_g, ln2_b,
           w_route_group, w_route_expert, w1, w3, w2):
    batch, seq, d = x_prompt.shape
    dec_batch, dec_seq, _ = x_sample.shape
    depth = w_in.shape[0]
    past = cache_ka.shape[2]
    mp = batch * seq
    ms = dec_batch * dec_seq
    m = mp + ms
    alpha = (2.0 * depth) ** 0.25
    width = N_HEADS * HEAD_DIM
    assert seq % QB == 0 and seq >= 2 * WIN and mp % dec_seq == 0 and width == 2048

    in_widths = (width, HEAD_DIM, HEAD_DIM, N_IDX_HEADS * IDX_DIM, IDX_DIM, N_IDX_HEADS, width, width, width)
    splits = [int(s) for s in np.cumsum(in_widths)[:-1]]
    wqa, wka, wva, wqi, wki, wwi, wqb, wkb, wvb = jnp.split(w_in, splits, axis=-1)
    w_q = jnp.concatenate([wqa, wqb, wqi], axis=-1).astype(BF16)
    pad = jnp.zeros(wwi.shape[:-1] + (LANES - N_IDX_HEADS,), w_in.dtype)
    w_kv = jnp.concatenate([wkb, wvb, wka, wva, wki, wki, wwi, pad], axis=-1).astype(BF16)
    n_kv = w_kv.shape[-1]
    cols_q = {"qa": 0, "qb": 1, "qi": 4}
    cols_kv = {"kb": 0, "vb": 1, "ka": 32, "va": 33, "kx": 34, "wi": 35}
    cols = {**cols_q, **cols_kv}
    w_out_b = w_out.astype(BF16)
    w1_b = w1.astype(BF16)
    w3_b = w3.astype(BF16)
    w2_b = w2.astype(BF16)
    w_r = jnp.concatenate([w_route_group, w_route_expert,
                           jnp.zeros((depth, d, LANES - N_GROUPS - N_EXPERTS), F32)], axis=-1).astype(BF16)
    vec3 = lambda a: a.reshape(depth, 1, a.shape[-1])
    lq1, lk1, lq2, lk2, subg = map(vec3, (lambda_q1, lambda_k1, lambda_q2, lambda_k2, subln_g))
    g1, b1, g2, b2 = map(vec3, (ln1_g, ln1_b, ln2_g, ln2_b))
    lam_inits = jnp.asarray([0.8 - 0.6 * math.exp(-0.3 * l) for l in range(depth)], F32)

    dwin, cfar = _prompt_bias_tiles(rel_bias)
    sbias = _sample_bias(rel_bias, past, dec_seq)
    dwin_a, dwin_b = dwin[:N_HEADS], dwin[N_HEADS:]
    cfar_a, cfar_b = cfar[:N_HEADS], cfar[N_HEADS:]
    sb_a_c, sb_a_n = sbias[:N_HEADS, :, :past], sbias[:N_HEADS, :, past:]
    sb_b_c, sb_b_n = sbias[N_HEADS:, :, :past], sbias[N_HEADS:, :, past:]
    cache_kb2 = cache_kb.reshape(cache_kb.shape[:3] + (width,))
    cache_vb2 = cache_vb.reshape(cache_vb.shape[:3] + (width,))
    cache_kx2 = jnp.concatenate([cache_kidx, cache_kidx], axis=-1)

    tm_proj = _pick_tile(m, (768, 512, 256, 128, 64, 32))
    tm_out = _pick_tile(m, (528, 384, 256, 128, 64, 32))
    tm_ln2 = _pick_tile(m, (256, 128, 64, 32))
    tm_moe = 256 if m >= 1024 else 32
    y_rows = 2 * m + (2 * m // tm_moe + N_EXPERTS) * tm_moe

    x0 = jnp.concatenate([x_prompt.reshape(mp, d), x_sample.reshape(ms, d)], axis=0)
    rows0 = jnp.zeros((depth, m, n_kv), F32)

    def layer(carry, l):
        x, xb, rows_all = carry
        lidx = jnp.reshape(l, (1,)).astype(jnp.int32)
        projq = _proj(lidx, xb, w_q, tm_proj, 512)[0]
        pkv_b, rows_all = _proj(lidx, xb, w_kv, tm_proj, 512, rows_all)
        oa = _dsa_prompt(lidx, projq, pkv_b, rows_all, dwin_a, cfar_a, batch, seq, m, cols)
        ob = _diff_prompt(lidx, projq, pkv_b, dwin_b, cfar_b, lam_inits, lq1, lk1, lq2, lk2, subg,
                          batch, seq, m, cols)
        oa = _dsa_sample(lidx, projq, pkv_b, rows_all, cache_ka, cache_va, cache_kx2, sb_a_c, sb_a_n, oa, mp, cols)
        ob = _diff_sample(lidx, projq, pkv_b, cache_kb2, cache_vb2, sb_b_c, sb_b_n, lam_inits,
                          lq1, lk1, lq2, lk2, subg, ob, mp, cols)
        h1, route = _out_ln(lidx, oa, ob, w_out_b, x, g1, b1, w_r, alpha, tm_out, 512)
        src3, dst3, gate128, tile_expert, n_used = _dispatch_plan(route, tm_moe)
        meta = jnp.stack([lidx[0], n_used])
        y = _moe(meta, tile_expert, src3, dst3, gate128, h1, w1_b, w3_b, w2_b, tm_moe, y_rows)
        h2, h2b = _ln2(lidx, h1, y, g2, b2, alpha, tm_ln2)
        return (h2, h2b, rows_all), None

    (y, _, rows_all), _ = lax.scan(layer, (x0, x0.astype(BF16), rows0), jnp.arange(depth, dtype=jnp.int32))
    heads = (N_HEADS, 2 * SUB_DIM)

    def split(c0, n, tail):
        a = rows_all[:, :, c0:c0 + n]
        return (a[:, :mp].reshape((depth, batch, seq) + tail),
                a[:, mp:].reshape((depth, dec_batch, dec_seq) + tail))

    p_kb, s_kb = split(0, width, heads)
    p_vb, s_vb = split(width, width, heads)
    p_ka, s_ka = split(2 * width, HEAD_DIM, (HEAD_DIM,))
    p_va, s_va = split(2 * width + HEAD_DIM, HEAD_DIM, (HEAD_DIM,))
    p_kx, s_kx = split(2 * width + 2 * HEAD_DIM, IDX_DIM, (IDX_DIM,))
    y_prompt = y[:mp].reshape(batch, seq, d)
    y_sample = y[mp:].reshape(dec_batch, dec_seq, d)
    return (y_prompt, y_sample, p_ka, p_va, p_kx, p_kb, p_vb, s_ka, s_va, s_kx, s_kb, s_vb)
```

```python
import functools
import math

import numpy as np
import jax
import jax.numpy as jnp
from jax import lax
from jax.experimental import pallas as pl
from jax.experimental.pallas import tpu as pltpu

CHUNK = 64
HEAD_DIM = 128
N_HEADS = 16
N_IDX_HEADS = 16
IDX_DIM = 64
SUB_DIM = 64
TOPK_MAX = 256
NUM_BUCKETS = 32
MAX_DISTANCE = 128
N_GROUPS = 4
EXPERTS_PER_GROUP = 4
N_EXPERTS = 16
LN_EPS = 1e-5
RMS_EPS = 1e-5

LANES = 128
QB = 128
WIN = 2 * QB
FAR_STEP = 512
HEAD_UNROLL = 4
NEG = -1e30
INT_MIN = -2147483648
LOG2E = 1.4426950408889634
VMEM_LIMIT = 56 * 1024 * 1024

F32 = jnp.float32
BF16 = jnp.bfloat16


def _cparams(*sem):
    return pltpu.CompilerParams(dimension_semantics=sem, vmem_limit_bytes=VMEM_LIMIT)


def _dot_nt(a, b):
    return lax.dot_general(a, b, (((1,), (1,)), ((), ())), preferred_element_type=F32)


def _dot(a, b):
    return jnp.dot(a, b, preferred_element_type=F32)


def _bucket_np(rel):
    nb = NUM_BUCKETS // 2
    ret = np.where(rel > 0, nb, 0)
    n = np.abs(rel)
    max_exact = nb // 2
    nf = np.maximum(n, 1).astype(np.float32)
    frac = np.log(nf / np.float32(max_exact)) / np.float32(math.log(MAX_DISTANCE / max_exact))
    large = max_exact + (frac.astype(np.float32) * np.float32(nb - max_exact)).astype(np.int32)
    large = np.minimum(large, nb - 1)
    return (ret + np.where(n < max_exact, n, large)).astype(np.int32)


def _bias_lookup(rel_bias, bucket):
    onehot = (jnp.asarray(bucket.reshape(-1, 1)) == jnp.arange(NUM_BUCKETS, dtype=jnp.int32)[None, :]).astype(F32)
    vals = jnp.dot(onehot, rel_bias.astype(F32), precision=lax.Precision.HIGHEST)
    return jnp.transpose(vals).reshape((rel_bias.shape[1],) + bucket.shape)


def _prompt_bias_tiles(rel_bias):
    ql = np.arange(QB)[:, None]
    c = np.arange(WIN)[None, :]
    rel = c - QB - ql
    kl = c - QB
    adm = (c < QB) | ((kl // CHUNK) <= (ql // CHUNK))
    tiles = _bias_lookup(rel_bias, _bucket_np(rel)) * LOG2E
    tiles = jnp.where(jnp.asarray(adm)[None], tiles, NEG)
    far = rel_bias[int(_bucket_np(np.array([-(QB + 1)]))[0])] * LOG2E
    return tiles.astype(F32), far.astype(F32)


def _sample_bias(rel_bias, past_len, dec_seq):
    q = past_len + np.arange(dec_seq)[:, None]
    k = np.arange(past_len + dec_seq)[None, :]
    return (_bias_lookup(rel_bias, _bucket_np(k - q)) * LOG2E).astype(F32)


def _proj_body(l_ref, x_ref, w_ref, ob_ref, *of_ref):
    acc = _dot(x_ref[...], w_ref[...])
    ob_ref[...] = acc.astype(BF16)
    if of_ref:
        of_ref[0][...] = acc


def _proj(lidx, xb, w, tm, tn, with_f32):
    m, d = xb.shape
    n = w.shape[-1]
    out_shape = [jax.ShapeDtypeStruct((m, n), BF16)]
    out_specs = [pl.BlockSpec((tm, tn), lambda i, j, l: (i, j))]
    if with_f32:
        out_shape.append(jax.ShapeDtypeStruct((m, n), F32))
        out_specs.append(pl.BlockSpec((tm, tn), lambda i, j, l: (i, j)))
    return pl.pallas_call(
        _proj_body,
        grid_spec=pltpu.PrefetchScalarGridSpec(
            num_scalar_prefetch=1, grid=(m // tm, n // tn),
            in_specs=[pl.BlockSpec((tm, d), lambda i, j, l: (i, 0)),
                      pl.BlockSpec((None, d, tn), lambda i, j, l: (l[0], 0, j))],
            out_specs=out_specs),
        out_shape=out_shape,
        compiler_params=_cparams("parallel", "arbitrary"),
        name="in_proj_rows" if with_f32 else "in_proj",
    )(lidx, xb, w)


def _sortable(score):
    bits = pltpu.bitcast(score + 0.0, jnp.int32)
    return jnp.where(bits < 0, bits ^ jnp.int32(0x7FFFFFFF), bits)


def _kth_largest(keys, k):
    t = keys[0].shape[0]

    def step(it, thr):
        bit = jnp.left_shift(jnp.int32(1), 31 - it)
        cand = thr + bit
        cnt = jnp.zeros((t, 1), F32)
        for kk in keys:
            cnt = cnt + jnp.sum(jnp.where(kk >= cand, 1.0, 0.0), axis=-1, keepdims=True)
        return jnp.where(cnt >= k, cand, thr)

    return lax.fori_loop(0, 32, step, jnp.full((t, 1), INT_MIN, jnp.int32))


def _index_scores(qi, wi, kparts):
    t = qi.shape[0]
    lane = lax.broadcasted_iota(jnp.int32, (t, LANES), 1)
    w = wi * (N_IDX_HEADS ** -0.5 * IDX_DIM ** -0.5)
    acc = [jnp.zeros((t, kp.shape[0]), F32) for kp in kparts]
    for p in range(N_IDX_HEADS // 2):
        qp = qi[:, p * LANES:(p + 1) * LANES]
        halves = (jnp.where(lane < IDX_DIM, qp, jnp.zeros_like(qp)),
                  jnp.where(lane >= IDX_DIM, qp, jnp.zeros_like(qp)))
        for s, qh in enumerate(halves):
            h = 2 * p + s
            wh = w[:, h:h + 1]
            for j, kp in enumerate(kparts):
                acc[j] = acc[j] + jnp.maximum(_dot_nt(qh, kp), 0.0) * wh
    return acc


def _softmax2(parts, shifts):
    m = None
    for tp, c in zip(parts, shifts):
        mj = tp.max(axis=-1, keepdims=True)
        if c is not None:
            mj = mj + c
        m = mj if m is None else jnp.maximum(m, mj)
    es, l = [], None
    for tp, c in zip(parts, shifts):
        e = jnp.exp2(tp - (m if c is None else m - c))
        es.append(e)
        lj = e.sum(axis=-1, keepdims=True)
        l = lj if l is None else l + lj
    return es, 1.0 / l


def _lane_half_masks(q):
    lane = lax.broadcasted_iota(jnp.int32, q.shape, 1)
    z = jnp.zeros_like(q)
    return jnp.where(lane < SUB_DIM, q, z), jnp.where(lane >= SUB_DIM, q, z)


def _lambda(lq1, lk1, lq2, lk2, lam_init):
    return (jnp.exp(jnp.sum(lq1 * lk1, axis=-1, keepdims=True))
            - jnp.exp(jnp.sum(lq2 * lk2, axis=-1, keepdims=True)) + lam_init)


def _sub_ln(o, g, lam_init):
    return o * lax.rsqrt(jnp.mean(o * o, axis=-1, keepdims=True) + RMS_EPS) * g * (1.0 - lam_init)


def _far_buckets(nq, far_max):
    lens = list(range(0, far_max, FAR_STEP)) + [far_max]
    if len(lens) > 2 and lens[-1] - lens[-2] < FAR_STEP:
        del lens[-2]
    out, lo = [], 0
    for fl in lens:
        hi = min(nq, fl // QB + 2)
        if hi > lo:
            out.append((lo, hi, fl))
            lo = hi
    assert lo == nq
    return out


def _window(ref, prev0, diag0, cols=slice(None)):
    return jnp.concatenate([ref[pl.ds(prev0, QB), cols], ref[pl.ds(diag0, QB), cols]], axis=0)


def _dsa_block(fl, topk, i, cfar_ref, qa_ref, qi_ref, wi_ref, k_ref, v_ref, kx_ref, dwin_ref,
               o_ref, mfar_ref, mwin_ref):
    prev0 = pl.multiple_of(jnp.maximum(i - 1, 0) * QB, QB)
    diag0 = pl.multiple_of(i * QB, QB)

    kparts = ([kx_ref[0:fl, :]] if fl else []) + [_window(kx_ref, prev0, diag0)]
    scores = _index_scores(qi_ref[...], wi_ref[...], kparts)
    s_win = scores[-1]
    col_win = lax.broadcasted_iota(jnp.int32, s_win.shape, 1)
    row_win = lax.broadcasted_iota(jnp.int32, s_win.shape, 0)
    ok_win = ((col_win < QB + CHUNK) | (row_win >= CHUNK)) & ((col_win >= QB) | (i > 0))
    key_win = jnp.where(ok_win, _sortable(s_win), INT_MIN)
    keys = [key_win]
    if fl:
        s_far = scores[0]
        col_far = lax.broadcasted_iota(jnp.int32, s_far.shape, 1)
        ok_far = col_far < (i - 1) * QB
        key_far = jnp.where(ok_far, _sortable(s_far), INT_MIN)
        keys = [key_far, key_win]
    thr = _kth_largest(keys, topk)
    mwin_ref[...] = jnp.where(ok_win & (key_win >= thr), 0.0, NEG)
    if fl:
        mfar_ref[:, 0:fl] = jnp.where(ok_far & (key_far >= thr), 0.0, NEG)

    k_win = _window(k_ref, prev0, diag0)
    v_win = _window(v_ref, prev0, diag0)
    c1 = HEAD_DIM ** -0.5 * LOG2E

    def head(h, carry):
        off = pl.multiple_of(h * HEAD_DIM, HEAD_DIM)
        q = qa_ref[:, pl.ds(off, HEAD_DIM)]
        tw = _dot_nt(q, k_win) * c1 + (dwin_ref[h] + mwin_ref[...])
        if fl:
            tf = _dot_nt(q, k_ref[0:fl, :]) * c1 + mfar_ref[:, 0:fl]
            (ef, ew), rl = _softmax2([tf, tw], [cfar_ref[h], None])
            o = (_dot(ef.astype(BF16), v_ref[0:fl, :]) + _dot(ew.astype(BF16), v_win)) * rl
        else:
            (ew,), rl = _softmax2([tw], [None])
            o = _dot(ew.astype(BF16), v_win) * rl
        o_ref[:, pl.ds(off, HEAD_DIM)] = o.astype(o_ref.dtype)
        return carry

    lax.fori_loop(0, N_HEADS, head, 0, unroll=HEAD_UNROLL)


def _dsa_prompt_body(buckets, topk, l_ref, *refs):
    i = pl.program_id(1)
    for lo, hi, fl in buckets:
        pl.when((i >= lo) & (i < hi))(functools.partial(_dsa_block, fl, topk, i, *refs))


def _dsa_prompt(lidx, projq, pkv_b, pkv_f, dwin, cfar, batch, seq, m_total, cols):
    nq = seq // QB
    far_max = seq - WIN
    topk = min(TOPK_MAX, seq // 4)
    width = N_HEADS * HEAD_DIM
    return pl.pallas_call(
        functools.partial(_dsa_prompt_body, _far_buckets(nq, far_max), topk),
        grid_spec=pltpu.PrefetchScalarGridSpec(
            num_scalar_prefetch=1, grid=(batch, nq),
            in_specs=[
                pl.BlockSpec(memory_space=pltpu.SMEM),
                pl.BlockSpec((QB, width), lambda b, i, l: (b * nq + i, cols["qa"])),
                pl.BlockSpec((QB, N_IDX_HEADS * IDX_DIM), lambda b, i, l: (b * nq + i, cols["qi"])),
                pl.BlockSpec((QB, LANES), lambda b, i, l: (b * nq + i, cols["wi"])),
                pl.BlockSpec((seq, HEAD_DIM), lambda b, i, l: (b, cols["ka"])),
                pl.BlockSpec((seq, HEAD_DIM), lambda b, i, l: (b, cols["va"])),
                pl.BlockSpec((seq, LANES), lambda b, i, l: (b, cols["kx"])),
                pl.BlockSpec((N_HEADS, QB, WIN), lambda b, i, l: (0, 0, 0)),
            ],
            out_specs=pl.BlockSpec((QB, width), lambda b, i, l: (b * nq + i, 0)),
            scratch_shapes=[pltpu.VMEM((QB, far_max), F32), pltpu.VMEM((QB, WIN), F32)]),
        out_shape=jax.ShapeDtypeStruct((m_total, width), BF16),
        compiler_params=_cparams("parallel", "arbitrary"),
        name="dsa_prompt",
    )(lidx, cfar, projq, projq, pkv_f, pkv_b, pkv_b, pkv_b, dwin)


def _diff_block(fl, i, l_ref, cfar_ref, laminit_ref, qb_ref, kb_ref, vb_ref, dwin_ref,
                lq1_ref, lk1_ref, lq2_ref, lk2_ref, g_ref, o_ref, mfar_ref, mwin_ref):
    prev0 = pl.multiple_of(jnp.maximum(i - 1, 0) * QB, QB)
    diag0 = pl.multiple_of(i * QB, QB)
    lam_init = laminit_ref[l_ref[0]]
    lam = _lambda(lq1_ref[...], lk1_ref[...], lq2_ref[...], lk2_ref[...], lam_init)
    col_win = lax.broadcasted_iota(jnp.int32, (QB, WIN), 1)
    mwin_ref[...] = jnp.where((col_win >= QB) | (i > 0), 0.0, NEG)
    if fl:
        col_far = lax.broadcasted_iota(jnp.int32, (QB, fl), 1)
        mfar_ref[:, 0:fl] = jnp.where(col_far < (i - 1) * QB, 0.0, NEG)
    c1 = SUB_DIM ** -0.5 * LOG2E
    g = g_ref[...]

    def head(h, carry):
        off = pl.multiple_of(h * HEAD_DIM, HEAD_DIM)
        hc = pl.ds(off, HEAD_DIM)
        q1, q2 = _lane_half_masks(qb_ref[:, hc])
        k_win = _window(kb_ref, prev0, diag0, hc)
        v_win = _window(vb_ref, prev0, diag0, hc)
        bias_win = dwin_ref[h] + mwin_ref[...]
        if fl:
            k_far = kb_ref[0:fl, hc]
            shifts = [cfar_ref[h], None]
            (e1f, e1w), r1 = _softmax2([_dot_nt(q1, k_far) * c1 + mfar_ref[:, 0:fl],
                                        _dot_nt(q1, k_win) * c1 + bias_win], shifts)
            (e2f, e2w), r2 = _softmax2([_dot_nt(q2, k_far) * c1 + mfar_ref[:, 0:fl],
                                        _dot_nt(q2, k_win) * c1 + bias_win], shifts)
            r2 = r2 * lam
            o = (_dot((e1f * r1 - e2f * r2).astype(BF16), vb_ref[0:fl, hc])
                 + _dot((e1w * r1 - e2w * r2).astype(BF16), v_win))
        else:
            (e1w,), r1 = _softmax2([_dot_nt(q1, k_win) * c1 + bias_win], [None])
            (e2w,), r2 = _softmax2([_dot_nt(q2, k_win) * c1 + bias_win], [None])
            r2 = r2 * lam
            o = _dot((e1w * r1 - e2w * r2).astype(BF16), v_win)
        o_ref[:, hc] = _sub_ln(o, g, lam_init).astype(o_ref.dtype)
        return carry

    lax.fori_loop(0, N_HEADS, head, 0, unroll=HEAD_UNROLL)


def _diff_prompt_body(buckets, l_ref, *refs):
    i = pl.program_id(1)
    for lo, hi, fl in buckets:
        pl.when((i >= lo) & (i < hi))(functools.partial(_diff_block, fl, i, l_ref, *refs))


def _diff_prompt(lidx, projq, pkv_b, dwin, cfar, lam_inits, lq1, lk1, lq2, lk2, subg, batch, seq, m_total, cols):
    nq = seq // QB
    far_max = seq - WIN
    width = N_HEADS * HEAD_DIM
    vec = lambda n: pl.BlockSpec((None, 1, n), lambda b, i, l: (l[0], 0, 0))
    return pl.pallas_call(
        functools.partial(_diff_prompt_body, _far_buckets(nq, far_max)),
        grid_spec=pltpu.PrefetchScalarGridSpec(
            num_scalar_prefetch=1, grid=(batch, nq),
            in_specs=[
                pl.BlockSpec(memory_space=pltpu.SMEM),
                pl.BlockSpec(memory_space=pltpu.SMEM),
                pl.BlockSpec((QB, width), lambda b, i, l: (b * nq + i, cols["qb"])),
                pl.BlockSpec((seq, width), lambda b, i, l: (b, cols["kb"])),
                pl.BlockSpec((seq, width), lambda b, i, l: (b, cols["vb"])),
                pl.BlockSpec((N_HEADS, QB, WIN), lambda b, i, l: (0, 0, 0)),
                vec(SUB_DIM), vec(SUB_DIM), vec(SUB_DIM), vec(SUB_DIM), vec(HEAD_DIM),
            ],
            out_specs=pl.BlockSpec((QB, width), lambda b, i, l: (b * nq + i, 0)),
            scratch_shapes=[pltpu.VMEM((QB, far_max), F32), pltpu.VMEM((QB, WIN), F32)]),
        out_shape=jax.ShapeDtypeStruct((m_total, width), BF16),
        compiler_params=_cparams("parallel", "arbitrary"),
        name="diff_prompt",
    )(lidx, cfar, lam_inits, projq, pkv_b, pkv_b, dwin, lq1, lk1, lq2, lk2, subg)


def _dsa_sample_body(topk, l_ref, qa_ref, qi_ref, wi_ref, kn_ref, vn_ref, kxn_ref, kc_ref, vc_ref, kxc_ref,
                     bc_ref, bn_ref, oin_ref, o_ref, mc_ref, mn_ref):
    s_c, s_n = _index_scores(qi_ref[...], wi_ref[...], [kxc_ref[...].astype(BF16), kxn_ref[...]])
    key_c = _sortable(s_c)
    key_n = _sortable(s_n)
    thr = _kth_largest([key_c, key_n], topk)
    mc_ref[...] = jnp.where(key_c >= thr, 0.0, NEG)
    mn_ref[...] = jnp.where(key_n >= thr, 0.0, NEG)
    k_c = kc_ref[...].astype(BF16)
    v_c = vc_ref[...].astype(BF16)
    k_n = kn_ref[...]
    v_n = vn_ref[...]
    c1 = HEAD_DIM ** -0.5 * LOG2E

    def head(h, carry):
        off = pl.multiple_of(h * HEAD_DIM, HEAD_DIM)
        q = qa_ref[:, pl.ds(off, HEAD_DIM)]
        tc = _dot_nt(q, k_c) * c1 + (bc_ref[h] + mc_ref[...])
        tn = _dot_nt(q, k_n) * c1 + (bn_ref[h] + mn_ref[...])
        (ec, en), rl = _softmax2([tc, tn], [None, None])
        o = (_dot(ec.astype(BF16), v_c) + _dot(en.astype(BF16), v_n)) * rl
        o_ref[:, pl.ds(off, HEAD_DIM)] = o.astype(o_ref.dtype)
        return carry

    lax.fori_loop(0, N_HEADS, head, 0)


def _dsa_sample(lidx, projq, pkv_b, pkv_f, cache_ka, cache_va, cache_kx2, bias_c, bias_n, o_all, row0, cols):
    _, nb, past, _ = cache_ka.shape
    t = bias_c.shape[1]
    r0 = row0 // t
    topk = min(TOPK_MAX, (past + t) // 4)
    width = N_HEADS * HEAD_DIM
    w128 = lambda key: (lambda b, l: (r0 + b, cols[key]))
    cache = lambda n: pl.BlockSpec((None, None, past, n), lambda b, l: (l[0], b, 0, 0))
    return pl.pallas_call(
        functools.partial(_dsa_sample_body, topk),
        grid_spec=pltpu.PrefetchScalarGridSpec(
            num_scalar_prefetch=1, grid=(nb,),
            in_specs=[
                pl.BlockSpec((t, width), w128("qa")),
                pl.BlockSpec((t, N_IDX_HEADS * IDX_DIM), w128("qi")),
                pl.BlockSpec((t, LANES), w128("wi")),
                pl.BlockSpec((t, HEAD_DIM), w128("ka")),
                pl.BlockSpec((t, HEAD_DIM), w128("va")),
                pl.BlockSpec((t, LANES), w128("kx")),
                cache(HEAD_DIM), cache(HEAD_DIM), cache(LANES),
                pl.BlockSpec((N_HEADS, t, past), lambda b, l: (0, 0, 0)),
                pl.BlockSpec((N_HEADS, t, t), lambda b, l: (0, 0, 0)),
                pl.BlockSpec(memory_space=pl.ANY),
            ],
            out_specs=pl.BlockSpec((t, width), lambda b, l: (r0 + b, 0)),
            scratch_shapes=[pltpu.VMEM((t, past), F32), pltpu.VMEM((t, t), F32)]),
        out_shape=jax.ShapeDtypeStruct(o_all.shape, BF16),
        input_output_aliases={12: 0},
        compiler_params=_cparams("arbitrary"),
        name="dsa_sample",
    )(lidx, projq, projq, pkv_f, pkv_b, pkv_b, pkv_b, cache_ka, cache_va, cache_kx2, bias_c, bias_n, o_all)


def _diff_sample_body(l_ref, laminit_ref, qb_ref, kn_ref, vn_ref, kc_ref, vc_ref, bc_ref, bn_ref,
                      lq1_ref, lk1_ref, lq2_ref, lk2_ref, g_ref, oin_ref, o_ref):
    lam_init = laminit_ref[l_ref[0]]
    lam = _lambda(lq1_ref[...], lk1_ref[...], lq2_ref[...], lk2_ref[...], lam_init)
    c1 = SUB_DIM ** -0.5 * LOG2E
    g = g_ref[...]

    def head(h, carry):
        off = pl.multiple_of(h * HEAD_DIM, HEAD_DIM)
        hc = pl.ds(off, HEAD_DIM)
        q1, q2 = _lane_half_masks(qb_ref[:, hc])
        k_c = kc_ref[:, hc].astype(BF16)
        v_c = vc_ref[:, hc].astype(BF16)
        k_n = kn_ref[:, hc]
        v_n = vn_ref[:, hc]
        bc = bc_ref[h]
        bn = bn_ref[h]
        none2 = [None, None]
        (e1c, e1n), r1 = _softmax2([_dot_nt(q1, k_c) * c1 + bc, _dot_nt(q1, k_n) * c1 + bn], none2)
        (e2c, e2n), r2 = _softmax2([_dot_nt(q2, k_c) * c1 + bc, _dot_nt(q2, k_n) * c1 + bn], none2)
        r2 = r2 * lam
        o = (_dot((e1c * r1 - e2c * r2).astype(BF16), v_c) + _dot((e1n * r1 - e2n * r2).astype(BF16), v_n))
        o_ref[:, hc] = _sub_ln(o, g, lam_init).astype(o_ref.dtype)
        return carry

    lax.fori_loop(0, N_HEADS, head, 0)


def _diff_sample(lidx, projq, pkv_b, cache_kb, cache_vb, bias_c, bias_n, lam_inits, lq1, lk1, lq2, lk2, subg,
                 o_all, row0, cols):
    _, nb, past, width = cache_kb.shape
    t = bias_c.shape[1]
    r0 = row0 // t
    vec = lambda n: pl.BlockSpec((None, 1, n), lambda b, l: (l[0], 0, 0))
    cache = pl.BlockSpec((None, None, past, width), lambda b, l: (l[0], b, 0, 0))
    return pl.pallas_call(
        _diff_sample_body,
        grid_spec=pltpu.PrefetchScalarGridSpec(
            num_scalar_prefetch=1, grid=(nb,),
            in_specs=[
                pl.BlockSpec(memory_space=pltpu.SMEM),
                pl.BlockSpec((t, width), lambda b, l: (r0 + b, cols["qb"])),
                pl.BlockSpec((t, width), lambda b, l: (r0 + b, cols["kb"])),
                pl.BlockSpec((t, width), lambda b, l: (r0 + b, cols["vb"])),
                cache, cache,
                pl.BlockSpec((N_HEADS, t, past), lambda b, l: (0, 0, 0)),
                pl.BlockSpec((N_HEADS, t, t), lambda b, l: (0, 0, 0)),
                vec(SUB_DIM), vec(SUB_DIM), vec(SUB_DIM), vec(SUB_DIM), vec(HEAD_DIM),
                pl.BlockSpec(memory_space=pl.ANY),
            ],
            out_specs=pl.BlockSpec((t, width), lambda b, l: (r0 + b, 0))),
        out_shape=jax.ShapeDtypeStruct(o_all.shape, BF16),
        input_output_aliases={14: 0},
        compiler_params=_cparams("arbitrary"),
        name="diff_sample",
    )(lidx, lam_inits, projq, pkv_b, pkv_b, cache_kb, cache_vb, bias_c, bias_n, lq1, lk1, lq2, lk2, subg, o_all)


def _layer_norm_rows(z, g, b):
    mu = jnp.mean(z, axis=-1, keepdims=True)
    zc = z - mu
    var = jnp.mean(zc * zc, axis=-1, keepdims=True)
    return zc * lax.rsqrt(var + LN_EPS) * g + b


def _route(rl):
    lane = lax.broadcasted_iota(jnp.int32, rl.shape, 1).astype(F32)
    big = 1e6
    gl = jnp.where(lane < N_GROUPS, rl, NEG)
    gmax = gl.max(axis=-1, keepdims=True)
    gsum = jnp.sum(jnp.exp(gl - gmax), axis=-1, keepdims=True)
    g_val = 1.0 / gsum
    g_idx = jnp.min(jnp.where(gl == gmax, lane, big), axis=-1, keepdims=True)
    lo = N_GROUPS + g_idx * EXPERTS_PER_GROUP
    el = jnp.where((lane >= lo) & (lane < lo + EXPERTS_PER_GROUP), rl, NEG)
    m1 = el.max(axis=-1, keepdims=True)
    i1 = jnp.min(jnp.where(el == m1, lane, big), axis=-1, keepdims=True)
    el2 = jnp.where(lane == i1, NEG, el)
    m2 = el2.max(axis=-1, keepdims=True)
    i2 = jnp.min(jnp.where(el2 == m2, lane, big), axis=-1, keepdims=True)
    e2 = jnp.exp(m2 - m1)
    den = 1.0 / (1.0 + e2)
    w1 = den * g_val
    w2 = e2 * den * g_val
    out = jnp.where(lane == 0, i1 - N_GROUPS, 0.0)
    out = jnp.where(lane == 1, i2 - N_GROUPS, out)
    out = jnp.where(lane == 2, w1, out)
    out = jnp.where(lane == 3, w2, out)
    return out


def _out_ln_body(alpha, tn, rows, l_ref, oa_ref, ob_ref, wa_ref, wb_ref, x_ref, g_ref, b_ref, wr_ref,
                 h_ref, rt_ref):
    j = pl.program_id(1)
    acc = _dot(oa_ref[...], wa_ref[...]) + _dot(ob_ref[...], wb_ref[...])
    off = pl.multiple_of(j * tn, tn)
    h_ref[:, pl.ds(off, tn)] = alpha * x_ref[...] + acc

    @pl.when(j == pl.num_programs(1) - 1)
    def _():
        g = g_ref[...]
        b = b_ref[...]

        def chunk(c, carry):
            r = pl.multiple_of(c * rows, rows)
            hn = _layer_norm_rows(h_ref[pl.ds(r, rows), :], g, b)
            h_ref[pl.ds(r, rows), :] = hn
            rt_ref[pl.ds(r, rows), :] = _dot(hn.astype(BF16), wr_ref[...])
            return carry

        lax.fori_loop(0, h_ref.shape[0] // rows, chunk, 0)
        rt_ref[...] = _route(rt_ref[...])


def _out_ln(lidx, oa, ob, w_out, x, g, b, wr, alpha, tm, tn):
    m, half = oa.shape
    d = w_out.shape[-1]
    rows = next(r for r in (64, 48, 32, 16, 8) if tm % r == 0)
    return pl.pallas_call(
        functools.partial(_out_ln_body, alpha, tn, rows),
        grid_spec=pltpu.PrefetchScalarGridSpec(
            num_scalar_prefetch=1, grid=(m // tm, d // tn),
            in_specs=[
                pl.BlockSpec((tm, half), lambda i, j, l: (i, 0)),
                pl.BlockSpec((tm, half), lambda i, j, l: (i, 0)),
                pl.BlockSpec((None, half, tn), lambda i, j, l: (l[0], 0, j)),
                pl.BlockSpec((None, half, tn), lambda i, j, l: (l[0], 1, j)),
                pl.BlockSpec((tm, tn), lambda i, j, l: (i, j)),
                pl.BlockSpec((None, 1, d), lambda i, j, l: (l[0], 0, 0)),
                pl.BlockSpec((None, 1, d), lambda i, j, l: (l[0], 0, 0)),
                pl.BlockSpec((None, d, LANES), lambda i, j, l: (l[0], 0, 0)),
            ],
            out_specs=[pl.BlockSpec((tm, d), lambda i, j, l: (i, 0)),
                       pl.BlockSpec((tm, LANES), lambda i, j, l: (i, 0))]),
        out_shape=[jax.ShapeDtypeStruct((m, d), F32), jax.ShapeDtypeStruct((m, LANES), F32)],
        compiler_params=_cparams("parallel", "arbitrary"),
        name="out_proj_ln_route",
    )(lidx, oa, ob, w_out, w_out, x, g, b, wr)


def _moe_body(tm, meta_ref, te_ref, src_ref, nxt_ref, dst_ref, gate_ref, h_hbm, w1_ref, w3_ref, w2_ref,
              y_hbm, xbuf, obuf, gsem, ssem):
    t = pl.program_id(0)
    n_used = meta_ref[1]
    slot = lax.rem(t, 2)

    def gather(idx_ref, s):
        def row(r, carry):
            pltpu.make_async_copy(h_hbm.at[pl.ds(idx_ref[0, r], 1)], xbuf.at[s, pl.ds(r, 1)], gsem.at[s]).start()
            return carry
        lax.fori_loop(0, tm, row, 0, unroll=8)

    def wait_gather(s):
        pltpu.make_async_copy(h_hbm.at[pl.ds(0, tm)], xbuf.at[s], gsem.at[s]).wait()

    def wait_scatter(s):
        pltpu.make_async_copy(obuf.at[s], y_hbm.at[pl.ds(0, tm)], ssem.at[s]).wait()

    @pl.when(t == 0)
    def _():
        gather(src_ref, 0)

    @pl.when(t + 1 < n_used)
    def _():
        gather(nxt_ref, 1 - slot)

    @pl.when(t < n_used)
    def _():
        wait_gather(slot)

        @pl.when(t >= 2)
        def _():
            wait_scatter(slot)

        x = xbuf[slot].astype(BF16)
        u = _dot(x, w1_ref[...])
        v = _dot(x, w3_ref[...])
        a = (u * (1.0 / (1.0 + jnp.exp(-u))) * v).astype(BF16)
        obuf[slot] = _dot(a, w2_ref[...]) * gate_ref[:, 0:1]

        def row(r, carry):
            pltpu.make_async_copy(obuf.at[slot, pl.ds(r, 1)], y_hbm.at[pl.ds(dst_ref[0, r], 1)], ssem.at[slot]).start()
            return carry
        lax.fori_loop(0, tm, row, 0, unroll=8)

    @pl.when(t == pl.num_programs(0) - 1)
    def _():
        wait_scatter(lax.rem(n_used - 1, 2))

        @pl.when(n_used >= 2)
        def _():
            wait_scatter(lax.rem(n_used, 2))


def _moe(meta, tile_expert, src3, dst3, gate, h, w1, w3, w2, tm, y_rows):
    m, d = h.shape
    f = w1.shape[-1]
    nt = src3.shape[0]
    idx_spec = lambda imap: pl.BlockSpec((None, 1, tm), imap, memory_space=pltpu.SMEM)
    wspec = lambda shape: pl.BlockSpec((None, None) + shape, lambda t, meta, te: (meta[0], te[t], 0, 0),
                                       pipeline_mode=pl.Buffered(1))
    return pl.pallas_call(
        functools.partial(_moe_body, tm),
        grid_spec=pltpu.PrefetchScalarGridSpec(
            num_scalar_prefetch=2, grid=(nt,),
            in_specs=[
                idx_spec(lambda t, meta, te: (t, 0, 0)),
                idx_spec(lambda t, meta, te: (jnp.minimum(t + 1, nt - 1), 0, 0)),
                idx_spec(lambda t, meta, te: (t, 0, 0)),
                pl.BlockSpec((tm, LANES), lambda t, meta, te: (t, 0)),
                pl.BlockSpec(memory_space=pl.ANY),
                wspec((d, f)), wspec((d, f)), wspec((f, d)),
            ],
            out_specs=pl.BlockSpec(memory_space=pl.ANY),
            scratch_shapes=[pltpu.VMEM((2, tm, d), F32), pltpu.VMEM((2, tm, d), F32),
                            pltpu.SemaphoreType.DMA((2,)), pltpu.SemaphoreType.DMA((2,))]),
        out_shape=jax.ShapeDtypeStruct((y_rows, d), F32),
        compiler_params=_cparams("arbitrary"),
        name="moe_experts",
    )(meta, tile_expert, src3, src3, dst3, gate, h, w1, w3, w2)


def _dispatch_plan(route, tm):
    m = route.shape[0]
    npairs = 2 * m
    nt = npairs // tm + N_EXPERTS
    p = nt * tm
    e = route[:, 0:2].astype(jnp.int32).reshape(npairs)
    wgt = route[:, 2:4].reshape(npairs)
    order = jnp.argsort(e, stable=True).astype(jnp.int32)
    counts = jnp.sum((e[:, None] == jnp.arange(N_EXPERTS, dtype=jnp.int32)[None, :]).astype(jnp.int32), axis=0)
    padded = ((counts + tm - 1) // tm) * tm
    pad_end = jnp.cumsum(padded)
    pad_start = pad_end - padded
    cnt_start = jnp.cumsum(counts) - counts
    tile_start = jnp.arange(nt, dtype=jnp.int32) * tm
    tile_expert = jnp.minimum(jnp.sum((pad_end[None, :] <= tile_start[:, None]).astype(jnp.int32), axis=1),
                              N_EXPERTS - 1)
    slot = jnp.arange(p, dtype=jnp.int32)
    se = jnp.repeat(tile_expert, tm)
    rank = slot - pad_start[se]
    valid = (rank >= 0) & (rank < counts[se])
    pair = order[jnp.clip(cnt_start[se] + rank, 0, npairs - 1)]
    src = jnp.where(valid, pair // 2, 0)
    dst = jnp.where(valid, (pair % 2) * m + pair // 2, npairs + slot)
    gate = jnp.where(valid, wgt[pair], 0.0)
    n_used = (pad_end[-1] // tm).astype(jnp.int32)
    gate128 = jnp.broadcast_to(gate[:, None], (p, LANES))
    return src.reshape(nt, 1, tm), dst.reshape(nt, 1, tm), gate128, tile_expert.astype(jnp.int32), n_used


def _ln2_body(alpha, l_ref, h_ref, y0_ref, y1_ref, g_ref, b_ref, o_ref, ob_ref):
    z = alpha * h_ref[...] + y0_ref[...] + y1_ref[...]
    hn = _layer_norm_rows(z, g_ref[...], b_ref[...])
    o_ref[...] = hn
    ob_ref[...] = hn.astype(BF16)


def _ln2(lidx, h, y, g, b, alpha, tm):
    m, d = h.shape
    nblk = m // tm
    return pl.pallas_call(
        functools.partial(_ln2_body, alpha),
        grid_spec=pltpu.PrefetchScalarGridSpec(
            num_scalar_prefetch=1, grid=(nblk,),
            in_specs=[
                pl.BlockSpec((tm, d), lambda i, l: (i, 0)),
                pl.BlockSpec((tm, d), lambda i, l: (i, 0)),
                pl.BlockSpec((tm, d), lambda i, l: (i + nblk, 0)),
                pl.BlockSpec((None, 1, d), lambda i, l: (l[0], 0, 0)),
                pl.BlockSpec((None, 1, d), lambda i, l: (l[0], 0, 0)),
            ],
            out_specs=[pl.BlockSpec((tm, d), lambda i, l: (i, 0)),
                       pl.BlockSpec((tm, d), lambda i, l: (i, 0))]),
        out_shape=[jax.ShapeDtypeStruct((m, d), F32), jax.ShapeDtypeStruct((m, d), BF16)],
        compiler_params=_cparams("parallel"),
        name="moe_combine_ln",
    )(lidx, h, y, y, g, b)


def _pick_tile(m, candidates):
    for c in candidates:
        if m % c == 0:
            return c
    raise ValueError(f"no row tile for {m} rows")


def kernel(x_prompt, x_sample, cache_ka, cache_va, cache_kidx, cache_kb, cache_vb, rel_bias, w_in, w_out,
           lambda_q1, lambda_k1, lambda_q2, lambda_k2, subln_g, ln1_g, ln1_b, ln2_g, ln2_b,
           w_route_group, w_route_expert, w1, w3, w2):
    batch, seq, d = x_prompt.shape
    dec_batch, dec_seq, _ = x_sample.shape
    depth = w_in.shape[0]
    past = cache_ka.shape[2]
    mp = batch * seq
    ms = dec_batch * dec_seq
    m = mp + ms
    alpha = (2.0 * depth) ** 0.25
    width = N_HEADS * HEAD_DIM
    assert seq % QB == 0 and seq >= 2 * WIN and mp % dec_seq == 0 and width == 2048

    in_widths = (width, HEAD_DIM, HEAD_DIM, N_IDX_HEADS * IDX_DIM, IDX_DIM, N_IDX_HEADS, width, width, width)
    splits = [int(s) for s in np.cumsum(in_widths)[:-1]]
    wqa, wka, wva, wqi, wki, wwi, wqb, wkb, wvb = jnp.split(w_in, splits, axis=-1)
    w_q = jnp.concatenate([wqa, wqb, wqi], axis=-1).astype(BF16)
    pad = jnp.zeros(wwi.shape[:-1] + (LANES - N_IDX_HEADS,), w_in.dtype)
    w_kv = jnp.concatenate([wkb, wvb, wka, wva, wki, wki, wwi, pad], axis=-1).astype(BF16)
    n_kv = w_kv.shape[-1]
    cols_q = {"qa": 0, "qb": 1, "qi": 4}
    cols_kv = {"kb": 0, "vb": 1, "ka": 32, "va": 33, "kx": 34, "wi": 35}
    cols = {**cols_q, **cols_kv}
    w_out_b = w_out.astype(BF16)
    w1_b = w1.astype(BF16)
    w3_b = w3.astype(BF16)
    w2_b = w2.astype(BF16)
    w_r = jnp.concatenate([w_route_group, w_route_expert,
                           jnp.zeros((depth, d, LANES - N_GROUPS - N_EXPERTS), F32)], axis=-1).astype(BF16)
    vec3 = lambda a: a.reshape(depth, 1, a.shape[-1])
    lq1, lk1, lq2, lk2, subg = map(vec3, (lambda_q1, lambda_k1, lambda_q2, lambda_k2, subln_g))
    g1, b1, g2, b2 = map(vec3, (ln1_g, ln1_b, ln2_g, ln2_b))
    lam_inits = jnp.asarray([0.8 - 0.6 * math.exp(-0.3 * l) for l in range(depth)], F32)

    dwin, cfar = _prompt_bias_tiles(rel_bias)
    sbias = _sample_bias(rel_bias, past, dec_seq)
    dwin_a, dwin_b = dwin[:N_HEADS], dwin[N_HEADS:]
    cfar_a, cfar_b = cfar[:N_HEADS], cfar[N_HEADS:]
    sb_a_c, sb_a_n = sbias[:N_HEADS, :, :past], sbias[:N_HEADS, :, past:]
    sb_b_c, sb_b_n = sbias[N_HEADS:, :, :past], sbias[N_HEADS:, :, past:]
    cache_kb2 = cache_kb.reshape(cache_kb.shape[:3] + (width,))
    cache_vb2 = cache_vb.reshape(cache_vb.shape[:3] + (width,))
    cache_kx2 = jnp.concatenate([cache_kidx, cache_kidx], axis=-1)

    tm_proj = _pick_tile(m, (768, 512, 256, 128, 64, 32))
    tm_out = _pick_tile(m, (528, 384, 256, 128, 64, 32))
    tm_ln2 = _pick_tile(m, (256, 128, 64, 32))
    tm_moe = 256 if m >= 1024 else 32
    y_rows = 2 * m + (2 * m // tm_moe + N_EXPERTS) * tm_moe

    x0 = jnp.concatenate([x_prompt.reshape(mp, d), x_sample.reshape(ms, d)], axis=0)
    x, xb = x0, x0.astype(BF16)
    rows = []
    for layer in range(depth):
        lidx = jnp.full((1,), layer, jnp.int32)
        projq = _proj(lidx, xb, w_q, tm_proj, 512, False)[0]
        pkv_b, pkv_f = _proj(lidx, xb, w_kv, tm_proj, 512, True)
        oa = _dsa_prompt(lidx, projq, pkv_b, pkv_f, dwin_a, cfar_a, batch, seq, m, cols)
        ob = _diff_prompt(lidx, projq, pkv_b, dwin_b, cfar_b, lam_inits, lq1, lk1, lq2, lk2, subg,
                          batch, seq, m, cols)
        oa = _dsa_sample(lidx, projq, pkv_b, pkv_f, cache_ka, cache_va, cache_kx2, sb_a_c, sb_a_n, oa, mp, cols)
        ob = _diff_sample(lidx, projq, pkv_b, cache_kb2, cache_vb2, sb_b_c, sb_b_n, lam_inits,
                          lq1, lk1, lq2, lk2, subg, ob, mp, cols)
        h1, route = _out_ln(lidx, oa, ob, w_out_b, x, g1, b1, w_r, alpha, tm_out, 512)
        src3, dst3, gate128, tile_expert, n_used = _dispatch_plan(route, tm_moe)
        meta = jnp.stack([lidx[0], n_used])
        y = _moe(meta, tile_expert, src3, dst3, gate128, h1, w1_b, w3_b, w2_b, tm_moe, y_rows)
        x, xb = _ln2(lidx, h1, y, g2, b2, alpha, tm_ln2)
        rows.append(pkv_f)
    y = x
    heads = (N_HEADS, 2 * SUB_DIM)

    def split(c0, n, tail):
        return (jnp.stack([r[:mp, c0:c0 + n] for r in rows]).reshape((depth, batch, seq) + tail),
                jnp.stack([r[mp:, c0:c0 + n] for r in rows]).reshape((depth, dec_batch, dec_seq) + tail))

    p_kb, s_kb = split(0, width, heads)
    p_vb, s_vb = split(width, width, heads)
    p_ka, s_ka = split(2 * width, HEAD_DIM, (HEAD_DIM,))
    p_va, s_va = split(2 * width + HEAD_DIM, HEAD_DIM, (HEAD_DIM,))
    p_kx, s_kx = split(2 * width + 2 * HEAD_DIM, IDX_DIM, (IDX_DIM,))
    y_prompt = y[:mp].reshape(batch, seq, d)
    y_sample = y[mp:].reshape(dec_batch, dec_seq, d)
    return (y_prompt, y_sample, p_ka, p_va, p_kx, p_kb, p_vb, s_ka, s_va, s_kx, s_kb, s_vb)
```

```python
import functools
import math

import numpy as np
import jax
import jax.numpy as jnp
from jax import lax
from jax.experimental import pallas as pl
from jax.experimental.pallas import tpu as pltpu

CHUNK = 64
HEAD_DIM = 128
N_HEADS = 16
N_IDX_HEADS = 16
IDX_DIM = 64
SUB_DIM = 64
TOPK_MAX = 256
NUM_BUCKETS = 32
MAX_DISTANCE = 128
N_GROUPS = 4
EXPERTS_PER_GROUP = 4
N_EXPERTS = 16
LN_EPS = 1e-5
RMS_EPS = 1e-5

LANES = 128
QB = 128
WIN = 2 * QB
FAR_STEP = 512
HEAD_UNROLL = 4
NEG = -1e30
INT_MIN = -2147483648
LOG2E = 1.4426950408889634
VMEM_LIMIT = 56 * 1024 * 1024

F32 = jnp.float32
BF16 = jnp.bfloat16


def _cparams(*sem):
    return pltpu.CompilerParams(dimension_semantics=sem, vmem_limit_bytes=VMEM_LIMIT)


def _dot_nt(a, b):
    return lax.dot_general(a, b, (((1,), (1,)), ((), ())), preferred_element_type=F32)


def _dot(a, b):
    return jnp.dot(a, b, preferred_element_type=F32)


def _bucket_np(rel):
    nb = NUM_BUCKETS // 2
    ret = np.where(rel > 0, nb, 0)
    n = np.abs(rel)
    max_exact = nb // 2
    nf = np.maximum(n, 1).astype(np.float32)
    frac = np.log(nf / np.float32(max_exact)) / np.float32(math.log(MAX_DISTANCE / max_exact))
    large = max_exact + (frac.astype(np.float32) * np.float32(nb - max_exact)).astype(np.int32)
    large = np.minimum(large, nb - 1)
    return (ret + np.where(n < max_exact, n, large)).astype(np.int32)


def _bias_lookup(rel_bias, bucket):
    onehot = (jnp.asarray(bucket.reshape(-1, 1)) == jnp.arange(NUM_BUCKETS, dtype=jnp.int32)[None, :]).astype(F32)
    vals = jnp.dot(onehot, rel_bias.astype(F32), precision=lax.Precision.HIGHEST)
    return jnp.transpose(vals).reshape((rel_bias.shape[1],) + bucket.shape)


def _prompt_bias_tiles(rel_bias):
    ql = np.arange(QB)[:, None]
    c = np.arange(WIN)[None, :]
    rel = c - QB - ql
    kl = c - QB
    adm = (c < QB) | ((kl // CHUNK) <= (ql // CHUNK))
    tiles = _bias_lookup(rel_bias, _bucket_np(rel)) * LOG2E
    tiles = jnp.where(jnp.asarray(adm)[None], tiles, NEG)
    far = rel_bias[int(_bucket_np(np.array([-(QB + 1)]))[0])] * LOG2E
    return tiles.astype(F32), far.astype(F32)


def _sample_bias(rel_bias, past_len, dec_seq):
    q = past_len + np.arange(dec_seq)[:, None]
    k = np.arange(past_len + dec_seq)[None, :]
    return (_bias_lookup(rel_bias, _bucket_np(k - q)) * LOG2E).astype(F32)


def _proj_body(l_ref, x_ref, w_ref, ob_ref, *of_ref):
    acc = _dot(x_ref[...], w_ref[...])
    ob_ref[...] = acc.astype(BF16)
    if of_ref:
        of_ref[0][...] = acc


def _proj(lidx, xb, w, tm, tn, with_f32):
    m, d = xb.shape
    n = w.shape[-1]
    out_shape = [jax.ShapeDtypeStruct((m, n), BF16)]
    out_specs = [pl.BlockSpec((tm, tn), lambda i, j, l: (i, j))]
    if with_f32:
        out_shape.append(jax.ShapeDtypeStruct((m, n), F32))
        out_specs.append(pl.BlockSpec((tm, tn), lambda i, j, l: (i, j)))
    return pl.pallas_call(
        _proj_body,
        grid_spec=pltpu.PrefetchScalarGridSpec(
            num_scalar_prefetch=1, grid=(m // tm, n // tn),
            in_specs=[pl.BlockSpec((tm, d), lambda i, j, l: (i, 0)),
                      pl.BlockSpec((None, d, tn), lambda i, j, l: (l[0], 0, j))],
            out_specs=out_specs),
        out_shape=out_shape,
        compiler_params=_cparams("parallel", "arbitrary"),
        name="in_proj_rows" if with_f32 else "in_proj",
    )(lidx, xb, w)


def _sortable(score):
    bits = pltpu.bitcast(score + 0.0, jnp.int32)
    return jnp.where(bits < 0, bits ^ jnp.int32(0x7FFFFFFF), bits)


def _kth_largest(keys, k):
    t = keys[0].shape[0]

    def step(it, thr):
        bit = jnp.left_shift(jnp.int32(1), 31 - it)
        cand = thr + bit
        cnt = jnp.zeros((t, 1), F32)
        for kk in keys:
            cnt = cnt + jnp.sum(jnp.where(kk >= cand, 1.0, 0.0), axis=-1, keepdims=True)
        return jnp.where(cnt >= k, cand, thr)

    return lax.fori_loop(0, 32, step, jnp.full((t, 1), INT_MIN, jnp.int32))


def _index_scores(qi, wi, kparts):
    t = qi.shape[0]
    lane = lax.broadcasted_iota(jnp.int32, (t, LANES), 1)
    w = wi * (N_IDX_HEADS ** -0.5 * IDX_DIM ** -0.5)
    acc = [jnp.zeros((t, kp.shape[0]), F32) for kp in kparts]
    for p in range(N_IDX_HEADS // 2):
        qp = qi[:, p * LANES:(p + 1) * LANES]
        halves = (jnp.where(lane < IDX_DIM, qp, jnp.zeros_like(qp)),
                  jnp.where(lane >= IDX_DIM, qp, jnp.zeros_like(qp)))
        for s, qh in enumerate(halves):
            h = 2 * p + s
            wh = w[:, h:h + 1]
            for j, kp in enumerate(kparts):
                acc[j] = acc[j] + jnp.maximum(_dot_nt(qh, kp), 0.0) * wh
    return acc


def _softmax2(parts, shifts):
    m = None
    for tp, c in zip(parts, shifts):
        mj = tp.max(axis=-1, keepdims=True)
        if c is not None:
            mj = mj + c
        m = mj if m is None else jnp.maximum(m, mj)
    es, l = [], None
    for tp, c in zip(parts, shifts):
        e = jnp.exp2(tp - (m if c is None else m - c))
        es.append(e)
        lj = e.sum(axis=-1, keepdims=True)
        l = lj if l is None else l + lj
    return es, 1.0 / l


def _lane_half_masks(q):
    lane = lax.broadcasted_iota(jnp.int32, q.shape, 1)
    z = jnp.zeros_like(q)
    return jnp.where(lane < SUB_DIM, q, z), jnp.where(lane >= SUB_DIM, q, z)


def _lambda(lq1, lk1, lq2, lk2, lam_init):
    return (jnp.exp(jnp.sum(lq1 * lk1, axis=-1, keepdims=True))
            - jnp.exp(jnp.sum(lq2 * lk2, axis=-1, keepdims=True)) + lam_init)


def _sub_ln(o, g, lam_init):
    return o * lax.rsqrt(jnp.mean(o * o, axis=-1, keepdims=True) + RMS_EPS) * g * (1.0 - lam_init)


def _far_buckets(nq, far_max):
    lens = list(range(0, far_max, FAR_STEP)) + [far_max]
    if len(lens) > 2 and lens[-1] - lens[-2] < FAR_STEP:
        del lens[-2]
    out, lo = [], 0
    for fl in lens:
        hi = min(nq, fl // QB + 2)
        if hi > lo:
            out.append((lo, hi, fl))
            lo = hi
    assert lo == nq
    return out


def _window(ref, prev0, diag0, cols=slice(None)):
    return jnp.concatenate([ref[pl.ds(prev0, QB), cols], ref[pl.ds(diag0, QB), cols]], axis=0)


def _dsa_block(fl, topk, i, cfar_ref, qa_ref, qi_ref, wi_ref, k_ref, v_ref, kx_ref, dwin_ref,
               o_ref, mfar_ref, mwin_ref):
    prev0 = pl.multiple_of(jnp.maximum(i - 1, 0) * QB, QB)
    diag0 = pl.multiple_of(i * QB, QB)

    kparts = ([kx_ref[0:fl, :]] if fl else []) + [_window(kx_ref, prev0, diag0)]
    scores = _index_scores(qi_ref[...], wi_ref[...], kparts)
    s_win = scores[-1]
    col_win = lax.broadcasted_iota(jnp.int32, s_win.shape, 1)
    row_win = lax.broadcasted_iota(jnp.int32, s_win.shape, 0)
    ok_win = ((col_win < QB + CHUNK) | (row_win >= CHUNK)) & ((col_win >= QB) | (i > 0))
    key_win = jnp.where(ok_win, _sortable(s_win), INT_MIN)
    keys = [key_win]
    if fl:
        s_far = scores[0]
        col_far = lax.broadcasted_iota(jnp.int32, s_far.shape, 1)
        ok_far = col_far < (i - 1) * QB
        key_far = jnp.where(ok_far, _sortable(s_far), INT_MIN)
        keys = [key_far, key_win]
    thr = _kth_largest(keys, topk)
    mwin_ref[...] = jnp.where(ok_win & (key_win >= thr), 0.0, NEG)
    if fl:
        mfar_ref[:, 0:fl] = jnp.where(ok_far & (key_far >= thr), 0.0, NEG)

    k_win = _window(k_ref, prev0, diag0)
    v_win = _window(v_ref, prev0, diag0)
    c1 = HEAD_DIM ** -0.5 * LOG2E

    def head(h, carry):
        off = pl.multiple_of(h * HEAD_DIM, HEAD_DIM)
        q = qa_ref[:, pl.ds(off, HEAD_DIM)]
        tw = _dot_nt(q, k_win) * c1 + (dwin_ref[h] + mwin_ref[...])
        if fl:
            tf = _dot_nt(q, k_ref[0:fl, :]) * c1 + mfar_ref[:, 0:fl]
            (ef, ew), rl = _softmax2([tf, tw], [cfar_ref[h], None])
            o = (_dot(ef.astype(BF16), v_ref[0:fl, :]) + _dot(ew.astype(BF16), v_win)) * rl
        else:
            (ew,), rl = _softmax2([tw], [None])
            o = _dot(ew.astype(BF16), v_win) * rl
        o_ref[:, pl.ds(off, HEAD_DIM)] = o.astype(o_ref.dtype)
        return carry

    lax.fori_loop(0, N_HEADS, head, 0, unroll=HEAD_UNROLL)


def _dsa_prompt_body(buckets, topk, l_ref, *refs):
    i = pl.program_id(1)
    for lo, hi, fl in buckets:
        pl.when((i >= lo) & (i < hi))(functools.partial(_dsa_block, fl, topk, i, *refs))


def _dsa_prompt(lidx, projq, pkv_b, pkv_f, dwin, cfar, batch, seq, m_total, cols):
    nq = seq // QB
    far_max = seq - WIN
    topk = min(TOPK_MAX, seq // 4)
    width = N_HEADS * HEAD_DIM
    return pl.pallas_call(
        functools.partial(_dsa_prompt_body, _far_buckets(nq, far_max), topk),
        grid_spec=pltpu.PrefetchScalarGridSpec(
            num_scalar_prefetch=1, grid=(batch, nq),
            in_specs=[
                pl.BlockSpec(memory_space=pltpu.SMEM),
                pl.BlockSpec((QB, width), lambda b, i, l: (b * nq + i, cols["qa"])),
                pl.BlockSpec((QB, N_IDX_HEADS * IDX_DIM), lambda b, i, l: (b * nq + i, cols["qi"])),
                pl.BlockSpec((QB, LANES), lambda b, i, l: (b * nq + i, cols["wi"])),
                pl.BlockSpec((seq, HEAD_DIM), lambda b, i, l: (b, cols["ka"])),
                pl.BlockSpec((seq, HEAD_DIM), lambda b, i, l: (b, cols["va"])),
                pl.BlockSpec((seq, LANES), lambda b, i, l: (b, cols["kx"])),
                pl.BlockSpec((N_HEADS, QB, WIN), lambda b, i, l: (0, 0, 0)),
            ],
            out_specs=pl.BlockSpec((QB, width), lambda b, i, l: (b * nq + i, 0)),
            scratch_shapes=[pltpu.VMEM((QB, far_max), F32), pltpu.VMEM((QB, WIN), F32)]),
        out_shape=jax.ShapeDtypeStruct((m_total, width), BF16),
        compiler_params=_cparams("parallel", "arbitrary"),
        name="dsa_prompt",
    )(lidx, cfar, projq, projq, pkv_f, pkv_b, pkv_b, pkv_b, dwin)


def _diff_block(fl, i, l_ref, cfar_ref, laminit_ref, qb_ref, kb_ref, vb_ref, dwin_ref,
                lq1_ref, lk1_ref, lq2_ref, lk2_ref, g_ref, o_ref, mfar_ref, mwin_ref):
    prev0 = pl.multiple_of(jnp.maximum(i - 1, 0) * QB, QB)
    diag0 = pl.multiple_of(i * QB, QB)
    lam_init = laminit_ref[l_ref[0]]
    lam = _lambda(lq1_ref[...], lk1_ref[...], lq2_ref[...], lk2_ref[...], lam_init)
    col_win = lax.broadcasted_iota(jnp.int32, (QB, WIN), 1)
    mwin_ref[...] = jnp.where((col_win >= QB) | (i > 0), 0.0, NEG)
    if fl:
        col_far = lax.broadcasted_iota(jnp.int32, (QB, fl), 1)
        mfar_ref[:, 0:fl] = jnp.where(col_far < (i - 1) * QB, 0.0, NEG)
    c1 = SUB_DIM ** -0.5 * LOG2E
    g = g_ref[...]

    def head(h, carry):
        off = pl.multiple_of(h * HEAD_DIM, HEAD_DIM)
        hc = pl.ds(off, HEAD_DIM)
        q1, q2 = _lane_half_masks(qb_ref[:, hc])
        k_win = _window(kb_ref, prev0, diag0, hc)
        v_win = _window(vb_ref, prev0, diag0, hc)
        bias_win = dwin_ref[h] + mwin_ref[...]
        if fl:
            k_far = kb_ref[0:fl, hc]
            shifts = [cfar_ref[h], None]
            (e1f, e1w), r1 = _softmax2([_dot_nt(q1, k_far) * c1 + mfar_ref[:, 0:fl],
                                        _dot_nt(q1, k_win) * c1 + bias_win], shifts)
            (e2f, e2w), r2 = _softmax2([_dot_nt(q2, k_far) * c1 + mfar_ref[:, 0:fl],
                                        _dot_nt(q2, k_win) * c1 + bias_win], shifts)
            r2 = r2 * lam
            o = (_dot((e1f * r1 - e2f * r2).astype(BF16), vb_ref[0:fl, hc])
                 + _dot((e1w * r1 - e2w * r2).astype(BF16), v_win))
        else:
            (e1w,), r1 = _softmax2([_dot_nt(q1, k_win) * c1 + bias_win], [None])
            (e2w,), r2 = _softmax2([_dot_nt(q2, k_win) * c1 + bias_win], [None])
            r2 = r2 * lam
            o = _dot((e1w * r1 - e2w * r2).astype(BF16), v_win)
        o_ref[:, hc] = _sub_ln(o, g, lam_init).astype(o_ref.dtype)
        return carry

    lax.fori_loop(0, N_HEADS, head, 0, unroll=HEAD_UNROLL)


def _diff_prompt_body(buckets, l_ref, *refs):
    i = pl.program_id(1)
    for lo, hi, fl in buckets:
        pl.when((i >= lo) & (i < hi))(functools.partial(_diff_block, fl, i, l_ref, *refs))


def _diff_prompt(lidx, projq, pkv_b, dwin, cfar, lam_inits, lq1, lk1, lq2, lk2, subg, batch, seq, m_total, cols):
    nq = seq // QB
    far_max = seq - WIN
    width = N_HEADS * HEAD_DIM
    vec = lambda n: pl.BlockSpec((None, 1, n), lambda b, i, l: (l[0], 0, 0))
    return pl.pallas_call(
        functools.partial(_diff_prompt_body, _far_buckets(nq, far_max)),
        grid_spec=pltpu.PrefetchScalarGridSpec(
            num_scalar_prefetch=1, grid=(batch, nq),
            in_specs=[
                pl.BlockSpec(memory_space=pltpu.SMEM),
                pl.BlockSpec(memory_space=pltpu.SMEM),
                pl.BlockSpec((QB, width), lambda b, i, l: (b * nq + i, cols["qb"])),
                pl.BlockSpec((seq, width), lambda b, i, l: (b, cols["kb"])),
                pl.BlockSpec((seq, width), lambda b, i, l: (b, cols["vb"])),
                pl.BlockSpec((N_HEADS, QB, WIN), lambda b, i, l: (0, 0, 0)),
                vec(SUB_DIM), vec(SUB_DIM), vec(SUB_DIM), vec(SUB_DIM), vec(HEAD_DIM),
            ],
            out_specs=pl.BlockSpec((QB, width), lambda b, i, l: (b * nq + i, 0)),
            scratch_shapes=[pltpu.VMEM((QB, far_max), F32), pltpu.VMEM((QB, WIN), F32)]),
        out_shape=jax.ShapeDtypeStruct((m_total, width), BF16),
        compiler_params=_cparams("parallel", "arbitrary"),
        name="diff_prompt",
    )(lidx, cfar, lam_inits, projq, pkv_b, pkv_b, dwin, lq1, lk1, lq2, lk2, subg)


def _dsa_sample_body(topk, l_ref, qa_ref, qi_ref, wi_ref, kn_ref, vn_ref, kxn_ref, kc_ref, vc_ref, kxc_ref,
                     bc_ref, bn_ref, oin_ref, o_ref, mc_ref, mn_ref):
    s_c, s_n = _index_scores(qi_ref[...], wi_ref[...], [kxc_ref[...].astype(BF16), kxn_ref[...]])
    key_c = _sortable(s_c)
    key_n = _sortable(s_n)
    thr = _kth_largest([key_c, key_n], topk)
    mc_ref[...] = jnp.where(key_c >= thr, 0.0, NEG)
    mn_ref[...] = jnp.where(key_n >= thr, 0.0, NEG)
    k_c = kc_ref[...].astype(BF16)
    v_c = vc_ref[...].astype(BF16)
    k_n = kn_ref[...]
    v_n = vn_ref[...]
    c1 = HEAD_DIM ** -0.5 * LOG2E

    def head(h, carry):
        off = pl.multiple_of(h * HEAD_DIM, HEAD_DIM)
        q = qa_ref[:, pl.ds(off, HEAD_DIM)]
        tc = _dot_nt(q, k_c) * c1 + (bc_ref[h] + mc_ref[...])
        tn = _dot_nt(q, k_n) * c1 + (bn_ref[h] + mn_ref[...])
        (ec, en), rl = _softmax2([tc, tn], [None, None])
        o = (_dot(ec.astype(BF16), v_c) + _dot(en.astype(BF16), v_n)) * rl
        o_ref[:, pl.ds(off, HEAD_DIM)] = o.astype(o_ref.dtype)
        return carry

    lax.fori_loop(0, N_HEADS, head, 0)


def _dsa_sample(lidx, projq, pkv_b, pkv_f, cache_ka, cache_va, cache_kx2, bias_c, bias_n, o_all, row0, cols):
    _, nb, past, _ = cache_ka.shape
    t = bias_c.shape[1]
    r0 = row0 // t
    topk = min(TOPK_MAX, (past + t) // 4)
    width = N_HEADS * HEAD_DIM
    w128 = lambda key: (lambda b, l: (r0 + b, cols[key]))
    cache = lambda n: pl.BlockSpec((None, None, past, n), lambda b, l: (l[0], b, 0, 0))
    return pl.pallas_call(
        functools.partial(_dsa_sample_body, topk),
        grid_spec=pltpu.PrefetchScalarGridSpec(
            num_scalar_prefetch=1, grid=(nb,),
            in_specs=[
                pl.BlockSpec((t, width), w128("qa")),
                pl.BlockSpec((t, N_IDX_HEADS * IDX_DIM), w128("qi")),
                pl.BlockSpec((t, LANES), w128("wi")),
                pl.BlockSpec((t, HEAD_DIM), w128("ka")),
                pl.BlockSpec((t, HEAD_DIM), w128("va")),
                pl.BlockSpec((t, LANES), w128("kx")),
                cache(HEAD_DIM), cache(HEAD_DIM), cache(LANES),
                pl.BlockSpec((N_HEADS, t, past), lambda b, l: (0, 0, 0)),
                pl.BlockSpec((N_HEADS, t, t), lambda b, l: (0, 0, 0)),
                pl.BlockSpec(memory_space=pl.ANY),
            ],
            out_specs=pl.BlockSpec((t, width), lambda b, l: (r0 + b, 0)),
            scratch_shapes=[pltpu.VMEM((t, past), F32), pltpu.VMEM((t, t), F32)]),
        out_shape=jax.ShapeDtypeStruct(o_all.shape, BF16),
        input_output_aliases={12: 0},
        compiler_params=_cparams("arbitrary"),
        name="dsa_sample",
    )(lidx, projq, projq, pkv_f, pkv_b, pkv_b, pkv_b, cache_ka, cache_va, cache_kx2, bias_c, bias_n, o_all)


def _diff_sample_body(l_ref, laminit_ref, qb_ref, kn_ref, vn_ref, kc_ref, vc_ref, bc_ref, bn_ref,
                      lq1_ref, lk1_ref, lq2_ref, lk2_ref, g_ref, oin_ref, o_ref):
    lam_init = laminit_ref[l_ref[0]]
    lam = _lambda(lq1_ref[...], lk1_ref[...], lq2_ref[...], lk2_ref[...], lam_init)
    c1 = SUB_DIM ** -0.5 * LOG2E
    g = g_ref[...]

    def head(h, carry):
        off = pl.multiple_of(h * HEAD_DIM, HEAD_DIM)
        hc = pl.ds(off, HEAD_DIM)
        q1, q2 = _lane_half_masks(qb_ref[:, hc])
        k_c = kc_ref[:, hc].astype(BF16)
        v_c = vc_ref[:, hc].astype(BF16)
        k_n = kn_ref[:, hc]
        v_n = vn_ref[:, hc]
        bc = bc_ref[h]
        bn = bn_ref[h]
        none2 = [None, None]
        (e1c, e1n), r1 = _softmax2([_dot_nt(q1, k_c) * c1 + bc, _dot_nt(q1, k_n) * c1 + bn], none2)
        (e2c, e2n), r2 = _softmax2([_dot_nt(q2, k_c) * c1 + bc, _dot_nt(q2, k_n) * c1 + bn], none2)
        r2 = r2 * lam
        o = (_dot((e1c * r1 - e2c * r2).astype(BF16), v_c) + _dot((e1n * r1 - e2n * r2).astype(BF16), v_n))
        o_ref[:, hc] = _sub_ln(o, g, lam_init).astype(o_ref.dtype)
        return carry

    lax.fori_loop(0, N_HEADS, head, 0)


def _diff_sample(lidx, projq, pkv_b, cache_kb, cache_vb, bias_c, bias_n, lam_inits, lq1, lk1, lq2, lk2, subg,
                 o_all, row0, cols):
    _, nb, past, width = cache_kb.shape
    t = bias_c.shape[1]
    r0 = row0 // t
    vec = lambda n: pl.BlockSpec((None, 1, n), lambda b, l: (l[0], 0, 0))
    cache = pl.BlockSpec((None, None, past, width), lambda b, l: (l[0], b, 0, 0))
    return pl.pallas_call(
        _diff_sample_body,
        grid_spec=pltpu.PrefetchScalarGridSpec(
            num_scalar_prefetch=1, grid=(nb,),
            in_specs=[
                pl.BlockSpec(memory_space=pltpu.SMEM),
                pl.BlockSpec((t, width), lambda b, l: (r0 + b, cols["qb"])),
                pl.BlockSpec((t, width), lambda b, l: (r0 + b, cols["kb"])),
                pl.BlockSpec((t, width), lambda b, l: (r0 + b, cols["vb"])),
                cache, cache,
                pl.BlockSpec((N_HEADS, t, past), lambda b, l: (0, 0, 0)),
                pl.BlockSpec((N_HEADS, t, t), lambda b, l: (0, 0, 0)),
                vec(SUB_DIM), vec(SUB_DIM), vec(SUB_DIM), vec(SUB_DIM), vec(HEAD_DIM),
                pl.BlockSpec(memory_space=pl.ANY),
            ],
            out_specs=pl.BlockSpec((t, width), lambda b, l: (r0 + b, 0))),
        out_shape=jax.ShapeDtypeStruct(o_all.shape, BF16),
        input_output_aliases={14: 0},
        compiler_params=_cparams("arbitrary"),
        name="diff_sample",
    )(lidx, lam_inits, projq, pkv_b, pkv_b, cache_kb, cache_vb, bias_c, bias_n, lq1, lk1, lq2, lk2, subg, o_all)


def _layer_norm_rows(z, g, b):
    mu = jnp.mean(z, axis=-1, keepdims=True)
    zc = z - mu
    var = jnp.mean(zc * zc, axis=-1, keepdims=True)
    return zc * lax.rsqrt(var + LN_EPS) * g + b


def _route(rl):
    lane = lax.broadcasted_iota(jnp.int32, rl.shape, 1).astype(F32)
    big = 1e6
    gl = jnp.where(lane < N_GROUPS, rl, NEG)
    gmax = gl.max(axis=-1, keepdims=True)
    gsum = jnp.sum(jnp.exp(gl - gmax), axis=-1, keepdims=True)
    g_val = 1.0 / gsum
    g_idx = jnp.min(jnp.where(gl == gmax, lane, big), axis=-1, keepdims=True)
    lo = N_GROUPS + g_idx * EXPERTS_PER_GROUP
    el = jnp.where((lane >= lo) & (lane < lo + EXPERTS_PER_GROUP), rl, NEG)
    m1 = el.max(axis=-1, keepdims=True)
    i1 = jnp.min(jnp.where(el == m1, lane, big), axis=-1, keepdims=True)
    el2 = jnp.where(lane == i1, NEG, el)
    m2 = el2.max(axis=-1, keepdims=True)
    i2 = jnp.min(jnp.where(el2 == m2, lane, big), axis=-1, keepdims=True)
    e2 = jnp.exp(m2 - m1)
    den = 1.0 / (1.0 + e2)
    w1 = den * g_val
    w2 = e2 * den * g_val
    out = jnp.where(lane == 0, i1 - N_GROUPS, 0.0)
    out = jnp.where(lane == 1, i2 - N_GROUPS, out)
    out = jnp.where(lane == 2, w1, out)
    out = jnp.where(lane == 3, w2, out)
    return out


def _out_ln_body(alpha, tn, rows, l_ref, oa_ref, ob_ref, wa_ref, wb_ref, x_ref, g_ref, b_ref, wr_ref,
                 h_ref, rt_ref):
    j = pl.program_id(1)
    acc = _dot(oa_ref[...], wa_ref[...]) + _dot(ob_ref[...], wb_ref[...])
    off = pl.multiple_of(j * tn, tn)
    h_ref[:, pl.ds(off, tn)] = alpha * x_ref[...] + acc

    @pl.when(j == pl.num_programs(1) - 1)
    def _():
        g = g_ref[...]
        b = b_ref[...]

        def chunk(c, carry):
            r = pl.multiple_of(c * rows, rows)
            hn = _layer_norm_rows(h_ref[pl.ds(r, rows), :], g, b)
            h_ref[pl.ds(r, rows), :] = hn
            rt_ref[pl.ds(r, rows), :] = _dot(hn.astype(BF16), wr_ref[...])
            return carry

        lax.fori_loop(0, h_ref.shape[0] // rows, chunk, 0)
        rt_ref[...] = _route(rt_ref[...])


def _out_ln(lidx, oa, ob, w_out, x, g, b, wr, alpha, tm, tn):
    m, half = oa.shape
    d = w_out.shape[-1]
    rows = next(r for r in (64, 48, 32, 16, 8) if tm % r == 0)
    return pl.pallas_call(
        functools.partial(_out_ln_body, alpha, tn, rows),
        grid_spec=pltpu.PrefetchScalarGridSpec(
            num_scalar_prefetch=1, grid=(m // tm, d // tn),
            in_specs=[
                pl.BlockSpec((tm, half), lambda i, j, l: (i, 0)),
                pl.BlockSpec((tm, half), lambda i, j, l: (i, 0)),
                pl.BlockSpec((None, half, tn), lambda i, j, l: (l[0], 0, j)),
                pl.BlockSpec((None, half, tn), lambda i, j, l: (l[0], 1, j)),
                pl.BlockSpec((tm, tn), lambda i, j, l: (i, j)),
                pl.BlockSpec((None, 1, d), lambda i, j, l: (l[0], 0, 0)),
                pl.BlockSpec((None, 1, d), lambda i, j, l: (l[0], 0, 0)),
                pl.BlockSpec((None, d, LANES), lambda i, j, l: (l[0], 0, 0)),
            ],
            out_specs=[pl.BlockSpec((tm, d), lambda i, j, l: (i, 0)),
                       pl.BlockSpec((tm, LANES), lambda i, j, l: (i, 0))]),
        out_shape=[jax.ShapeDtypeStruct((m, d), F32), jax.ShapeDtypeStruct((m, LANES), F32)],
        compiler_params=_cparams("parallel", "arbitrary"),
        name="out_proj_ln_route",
    )(lidx, oa, ob, w_out, w_out, x, g, b, wr)


def _moe_body(tm, spare_row, meta_ref, te_ref, src_ref, nxt_ref, dst_ref, prv_ref, gate_ref, h_hbm,
              w1_ref, w3_ref, w2_ref, y_hbm, xbuf, obuf, xb, gsem, ssem):
    t = pl.program_id(0)
    n_used = meta_ref[1]
    slot = lax.rem(t, 2)
    other = 1 - slot

    def wait_gather(s):
        pltpu.make_async_copy(h_hbm.at[pl.ds(0, tm)], xbuf.at[s], gsem.at[s]).wait()

    def wait_scatter(s):
        pltpu.make_async_copy(obuf.at[s], y_hbm.at[pl.ds(0, tm)], ssem.at[s]).wait()

    @pl.when(t == 0)
    def _():
        def row(r, carry):
            pltpu.make_async_copy(h_hbm.at[pl.ds(src_ref[0, r], 1)], xbuf.at[0, pl.ds(r, 1)], gsem.at[0]).start()
            return carry
        lax.fori_loop(0, tm, row, 0, unroll=8)
        obuf[...] = jnp.zeros_like(obuf)
        pltpu.make_async_copy(obuf.at[0], y_hbm.at[pl.ds(spare_row, tm)], ssem.at[0]).start()

    @pl.when(t < n_used)
    def _():
        wait_gather(slot)
        wait_scatter(slot)
        xb[...] = xbuf[slot].astype(BF16)
        for r in range(tm):
            pltpu.make_async_copy(h_hbm.at[pl.ds(nxt_ref[0, r], 1)], xbuf.at[other, pl.ds(r, 1)],
                                  gsem.at[other]).start()
        for r in range(tm):
            pltpu.make_async_copy(obuf.at[other, pl.ds(r, 1)], y_hbm.at[pl.ds(prv_ref[0, r], 1)],
                                  ssem.at[other]).start()
        x = xb[...]
        u = _dot(x, w1_ref[...])
        v = _dot(x, w3_ref[...])
        a = (u * (1.0 / (1.0 + jnp.exp(-u))) * v).astype(BF16)
        obuf[slot] = _dot(a, w2_ref[...]) * gate_ref[:, 0:1]

        @pl.when(t + 1 >= n_used)
        def _():
            def row(r, carry):
                pltpu.make_async_copy(obuf.at[slot, pl.ds(r, 1)], y_hbm.at[pl.ds(dst_ref[0, r], 1)],
                                      ssem.at[slot]).start()
                return carry
            lax.fori_loop(0, tm, row, 0, unroll=8)
            wait_gather(other)
            wait_scatter(other)
            wait_scatter(slot)


def _moe(meta, tile_expert, src3, dst3, prv3, gate, h, w1, w3, w2, tm, y_rows):
    m, d = h.shape
    f = w1.shape[-1]
    nt = src3.shape[0]
    spare_row = y_rows - tm
    idx_spec = lambda imap: pl.BlockSpec((None, 1, tm), imap, memory_space=pltpu.SMEM)
    wspec = lambda shape: pl.BlockSpec((None, None) + shape, lambda t, meta, te: (meta[0], te[t], 0, 0),
                                       pipeline_mode=pl.Buffered(1))
    return pl.pallas_call(
        functools.partial(_moe_body, tm, spare_row),
        grid_spec=pltpu.PrefetchScalarGridSpec(
            num_scalar_prefetch=2, grid=(nt,),
            in_specs=[
                idx_spec(lambda t, meta, te: (t, 0, 0)),
                idx_spec(lambda t, meta, te: (jnp.minimum(t + 1, nt - 1), 0, 0)),
                idx_spec(lambda t, meta, te: (t, 0, 0)),
                idx_spec(lambda t, meta, te: (t, 0, 0)),
                pl.BlockSpec((tm, LANES), lambda t, meta, te: (t, 0)),
                pl.BlockSpec(memory_space=pl.ANY),
                wspec((d, f)), wspec((d, f)), wspec((f, d)),
            ],
            out_specs=pl.BlockSpec(memory_space=pl.ANY),
            scratch_shapes=[pltpu.VMEM((2, tm, d), F32), pltpu.VMEM((2, tm, d), F32), pltpu.VMEM((tm, d), BF16),
                            pltpu.SemaphoreType.DMA((2,)), pltpu.SemaphoreType.DMA((2,))]),
        out_shape=jax.ShapeDtypeStruct((y_rows, d), F32),
        compiler_params=_cparams("arbitrary"),
        name="moe_experts",
    )(meta, tile_expert, src3, src3, dst3, prv3, gate, h, w1, w3, w2)


def _dispatch_plan(route, tm):
    m = route.shape[0]
    npairs = 2 * m
    nt = npairs // tm + N_EXPERTS
    p = nt * tm
    e = route[:, 0:2].astype(jnp.int32).reshape(npairs)
    wgt = route[:, 2:4].reshape(npairs)
    order = jnp.argsort(e, stable=True).astype(jnp.int32)
    counts = jnp.sum((e[:, None] == jnp.arange(N_EXPERTS, dtype=jnp.int32)[None, :]).astype(jnp.int32), axis=0)
    padded = ((counts + tm - 1) // tm) * tm
    pad_end = jnp.cumsum(padded)
    pad_start = pad_end - padded
    cnt_start = jnp.cumsum(counts) - counts
    tile_start = jnp.arange(nt, dtype=jnp.int32) * tm
    tile_expert = jnp.minimum(jnp.sum((pad_end[None, :] <= tile_start[:, None]).astype(jnp.int32), axis=1),
                              N_EXPERTS - 1)
    slot = jnp.arange(p, dtype=jnp.int32)
    se = jnp.repeat(tile_expert, tm)
    rank = slot - pad_start[se]
    valid = (rank >= 0) & (rank < counts[se])
    pair = order[jnp.clip(cnt_start[se] + rank, 0, npairs - 1)]
    src = jnp.where(valid, pair // 2, 0)
    dst = jnp.where(valid, (pair % 2) * m + pair // 2, npairs + slot)
    gate = jnp.where(valid, wgt[pair], 0.0)
    n_used = (pad_end[-1] // tm).astype(jnp.int32)
    gate128 = jnp.broadcast_to(gate[:, None], (p, LANES))
    dst3 = dst.reshape(nt, 1, tm)
    spare = (npairs + p + jnp.arange(tm, dtype=jnp.int32)).reshape(1, 1, tm)
    prv3 = jnp.concatenate([spare, dst3[:-1]], axis=0)
    return src.reshape(nt, 1, tm), dst3, prv3, gate128, tile_expert.astype(jnp.int32), n_used


def _ln2_body(alpha, l_ref, h_ref, y0_ref, y1_ref, g_ref, b_ref, o_ref, ob_ref):
    z = alpha * h_ref[...] + y0_ref[...] + y1_ref[...]
    hn = _layer_norm_rows(z, g_ref[...], b_ref[...])
    o_ref[...] = hn
    ob_ref[...] = hn.astype(BF16)


def _ln2(lidx, h, y, g, b, alpha, tm):
    m, d = h.shape
    nblk = m // tm
    return pl.pallas_call(
        functools.partial(_ln2_body, alpha),
        grid_spec=pltpu.PrefetchScalarGridSpec(
            num_scalar_prefetch=1, grid=(nblk,),
            in_specs=[
                pl.BlockSpec((tm, d), lambda i, l: (i, 0)),
                pl.BlockSpec((tm, d), lambda i, l: (i, 0)),
                pl.BlockSpec((tm, d), lambda i, l: (i + nblk, 0)),
                pl.BlockSpec((None, 1, d), lambda i, l: (l[0], 0, 0)),
                pl.BlockSpec((None, 1, d), lambda i, l: (l[0], 0, 0)),
            ],
            out_specs=[pl.BlockSpec((tm, d), lambda i, l: (i, 0)),
                       pl.BlockSpec((tm, d), lambda i, l: (i, 0))]),
        out_shape=[jax.ShapeDtypeStruct((m, d), F32), jax.ShapeDtypeStruct((m, d), BF16)],
        compiler_params=_cparams("parallel"),
        name="moe_combine_ln",
    )(lidx, h, y, y, g, b)


def _pick_tile(m, candidates):
    for c in candidates:
        if m % c == 0:
            return c
    raise ValueError(f"no row tile for {m} rows")


def kernel(x_prompt, x_sample, cache_ka, cache_va, cache_kidx, cache_kb, cache_vb, rel_bias, w_in, w_out,
           lambda_q1, lambda_k1, lambda_q2, lambda_k2, subln_g, ln1_g, ln1_b, ln2_g, ln2_b,
           w_route_group, w_route_expert, w1, w3, w2):
    batch, seq, d = x_prompt.shape
    dec_batch, dec_seq, _ = x_sample.shape
    depth = w_in.shape[0]
    past = cache_ka.shape[2]
    mp = batch * seq
    ms = dec_batch * dec_seq
    m = mp + ms
    alpha = (2.0 * depth) ** 0.25
    width = N_HEADS * HEAD_DIM
    assert seq % QB == 0 and seq >= 2 * WIN and mp % dec_seq == 0 and width == 2048

    in_widths = (width, HEAD_DIM, HEAD_DIM, N_IDX_HEADS * IDX_DIM, IDX_DIM, N_IDX_HEADS, width, width, width)
    splits = [int(s) for s in np.cumsum(in_widths)[:-1]]
    wqa, wka, wva, wqi, wki, wwi, wqb, wkb, wvb = jnp.split(w_in.astype(BF16), splits, axis=-1)
    w_q = jnp.concatenate([wqa, wqb, wqi], axis=-1)
    pad = jnp.zeros(wwi.shape[:-1] + (LANES - N_IDX_HEADS,), BF16)
    w_kv = jnp.concatenate([wkb, wvb, wka, wva, wki, wki, wwi, pad], axis=-1)
    n_kv = w_kv.shape[-1]
    cols_q = {"qa": 0, "qb": 1, "qi": 4}
    cols_kv = {"kb": 0, "vb": 1, "ka": 32, "va": 33, "kx": 34, "wi": 35}
    cols = {**cols_q, **cols_kv}
    w_out_b = w_out.astype(BF16)
    w1_b = w1.astype(BF16)
    w3_b = w3.astype(BF16)
    w2_b = w2.astype(BF16)
    w_r = jnp.concatenate([w_route_group, w_route_expert,
                           jnp.zeros((depth, d, LANES - N_GROUPS - N_EXPERTS), F32)], axis=-1).astype(BF16)
    vec3 = lambda a: a.reshape(depth, 1, a.shape[-1])
    lq1, lk1, lq2, lk2, subg = map(vec3, (lambda_q1, lambda_k1, lambda_q2, lambda_k2, subln_g))
    g1, b1, g2, b2 = map(vec3, (ln1_g, ln1_b, ln2_g, ln2_b))
    lam_inits = jnp.asarray([0.8 - 0.6 * math.exp(-0.3 * l) for l in range(depth)], F32)

    dwin, cfar = _prompt_bias_tiles(rel_bias)
    sbias = _sample_bias(rel_bias, past, dec_seq)
    dwin_a, dwin_b = dwin[:N_HEADS], dwin[N_HEADS:]
    cfar_a, cfar_b = cfar[:N_HEADS], cfar[N_HEADS:]
    sb_a_c, sb_a_n = sbias[:N_HEADS, :, :past], sbias[:N_HEADS, :, past:]
    sb_b_c, sb_b_n = sbias[N_HEADS:, :, :past], sbias[N_HEADS:, :, past:]
    cache_kb2 = cache_kb.astype(BF16).reshape(cache_kb.shape[:3] + (width,))
    cache_vb2 = cache_vb.astype(BF16).reshape(cache_vb.shape[:3] + (width,))
    cache_kx2 = jnp.concatenate([cache_kidx, cache_kidx], axis=-1)

    tm_proj = _pick_tile(m, (768, 512, 256, 128, 64, 32))
    tm_out = _pick_tile(m, (528, 384, 256, 128, 64, 32))
    tm_ln2 = _pick_tile(m, (256, 128, 64, 32))
    tm_moe = 256 if m >= 1024 else 32
    y_rows = 2 * m + (2 * m // tm_moe + N_EXPERTS) * tm_moe + 2 * tm_moe

    x0 = jnp.concatenate([x_prompt.reshape(mp, d), x_sample.reshape(ms, d)], axis=0)
    x, xb = x0, x0.astype(BF16)
    rows = []
    for layer in range(depth):
        lidx = jnp.full((1,), layer, jnp.int32)
        projq = _proj(lidx, xb, w_q, tm_proj, 512, False)[0]
        pkv_b, pkv_f = _proj(lidx, xb, w_kv, tm_proj, 512, True)
        oa = _dsa_prompt(lidx, projq, pkv_b, pkv_f, dwin_a, cfar_a, batch, seq, m, cols)
        ob = _diff_prompt(lidx, projq, pkv_b, dwin_b, cfar_b, lam_inits, lq1, lk1, lq2, lk2, subg,
                          batch, seq, m, cols)
        oa = _dsa_sample(lidx, projq, pkv_b, pkv_f, cache_ka, cache_va, cache_kx2, sb_a_c, sb_a_n, oa, mp, cols)
        ob = _diff_sample(lidx, projq, pkv_b, cache_kb2, cache_vb2, sb_b_c, sb_b_n, lam_inits,
                          lq1, lk1, lq2, lk2, subg, ob, mp, cols)
        h1, route = _out_ln(lidx, oa, ob, w_out_b, x, g1, b1, w_r, alpha, tm_out, 512)
        src3, dst3, prv3, gate128, tile_expert, n_used = _dispatch_plan(route, tm_moe)
        meta = jnp.stack([lidx[0], n_used])
        y = _moe(meta, tile_expert, src3, dst3, prv3, gate128, h1, w1_b, w3_b, w2_b, tm_moe, y_rows)
        x, xb = _ln2(lidx, h1, y, g2, b2, alpha, tm_ln2)
        rows.append(pkv_f)
    y = x
    heads = (N_HEADS, 2 * SUB_DIM)

    def split(c0, n, tail):
        return (jnp.stack([r[:mp, c0:c0 + n] for r in rows]).reshape((depth, batch, seq) + tail),
                jnp.stack([r[mp:, c0:c0 + n] for r in rows]).reshape((depth, dec_batch, dec_seq) + tail))

    p_kb, s_kb = split(0, width, heads)
    p_vb, s_vb = split(width, width, heads)
    p_ka, s_ka = split(2 * width, HEAD_DIM, (HEAD_DIM,))
    p_va, s_va = split(2 * width + HEAD_DIM, HEAD_DIM, (HEAD_DIM,))
    p_kx, s_kx = split(2 * width + 2 * HEAD_DIM, IDX_DIM, (IDX_DIM,))
    y_prompt = y[:mp].reshape(batch, seq, d)
    y_sample = y[mp:].reshape(dec_batch, dec_seq, d)
    return (y_prompt, y_sample, p_ka, p_va, p_kx, p_kb, p_vb, s_ka, s_va, s_kx, s_kb, s_vb)
```

```python
import functools
import math

import numpy as np
import jax
import jax.numpy as jnp
from jax import lax
from jax.experimental import pallas as pl
from jax.experimental.pallas import tpu as pltpu

CHUNK = 64
HEAD_DIM = 128
N_HEADS = 16
N_IDX_HEADS = 16
IDX_DIM = 64
SUB_DIM = 64
TOPK_MAX = 256
NUM_BUCKETS = 32
MAX_DISTANCE = 128
N_GROUPS = 4
EXPERTS_PER_GROUP = 4
N_EXPERTS = 16
LN_EPS = 1e-5
RMS_EPS = 1e-5

LANES = 128
QB = 128
WIN = 2 * QB
FAR_STEP = 512
HEAD_UNROLL = 4
NEG = -1e30
INT_MIN = -2147483648
LOG2E = 1.4426950408889634
VMEM_LIMIT = 56 * 1024 * 1024

F32 = jnp.float32
BF16 = jnp.bfloat16


def _cparams(*sem):
    return pltpu.CompilerParams(dimension_semantics=sem, vmem_limit_bytes=VMEM_LIMIT)


def _dot_nt(a, b):
    return lax.dot_general(a, b, (((1,), (1,)), ((), ())), preferred_element_type=F32)


def _dot(a, b):
    return jnp.dot(a, b, preferred_element_type=F32)


def _bucket_np(rel):
    nb = NUM_BUCKETS // 2
    ret = np.where(rel > 0, nb, 0)
    n = np.abs(rel)
    max_exact = nb // 2
    nf = np.maximum(n, 1).astype(np.float32)
    frac = np.log(nf / np.float32(max_exact)) / np.float32(math.log(MAX_DISTANCE / max_exact))
    large = max_exact + (frac.astype(np.float32) * np.float32(nb - max_exact)).astype(np.int32)
    large = np.minimum(large, nb - 1)
    return (ret + np.where(n < max_exact, n, large)).astype(np.int32)


def _bias_lookup(rel_bias, bucket):
    onehot = (jnp.asarray(bucket.reshape(-1, 1)) == jnp.arange(NUM_BUCKETS, dtype=jnp.int32)[None, :]).astype(F32)
    vals = jnp.dot(onehot, rel_bias.astype(F32), precision=lax.Precision.HIGHEST)
    return jnp.transpose(vals).reshape((rel_bias.shape[1],) + bucket.shape)


def _prompt_bias_tiles(rel_bias):
    ql = np.arange(QB)[:, None]
    c = np.arange(WIN)[None, :]
    rel = c - QB - ql
    kl = c - QB
    adm = (c < QB) | ((kl // CHUNK) <= (ql // CHUNK))
    tiles = _bias_lookup(rel_bias, _bucket_np(rel)) * LOG2E
    tiles = jnp.where(jnp.asarray(adm)[None], tiles, NEG)
    far = rel_bias[int(_bucket_np(np.array([-(QB + 1)]))[0])] * LOG2E
    return tiles.astype(F32), far.astype(F32)


def _sample_bias(rel_bias, past_len, dec_seq):
    q = past_len + np.arange(dec_seq)[:, None]
    k = np.arange(past_len + dec_seq)[None, :]
    return (_bias_lookup(rel_bias, _bucket_np(k - q)) * LOG2E).astype(F32)


def _proj_body(l_ref, x_ref, w_ref, ob_ref, *of_ref):
    acc = _dot(x_ref[...], w_ref[...])
    ob_ref[...] = acc.astype(BF16)
    if of_ref:
        of_ref[0][...] = acc


def _proj(lidx, xb, w, tm, tn, with_f32):
    m, d = xb.shape
    n = w.shape[-1]
    out_shape = [jax.ShapeDtypeStruct((m, n), BF16)]
    out_specs = [pl.BlockSpec((tm, tn), lambda i, j, l: (i, j))]
    if with_f32:
        out_shape.append(jax.ShapeDtypeStruct((m, n), F32))
        out_specs.append(pl.BlockSpec((tm, tn), lambda i, j, l: (i, j)))
    return pl.pallas_call(
        _proj_body,
        grid_spec=pltpu.PrefetchScalarGridSpec(
            num_scalar_prefetch=1, grid=(m // tm, n // tn),
            in_specs=[pl.BlockSpec((tm, d), lambda i, j, l: (i, 0)),
                      pl.BlockSpec((None, d, tn), lambda i, j, l: (l[0], 0, j))],
            out_specs=out_specs),
        out_shape=out_shape,
        compiler_params=_cparams("parallel", "arbitrary"),
        name="in_proj_rows" if with_f32 else "in_proj",
    )(lidx, xb, w)


def _sortable(score):
    bits = pltpu.bitcast(score + 0.0, jnp.int32)
    return jnp.where(bits < 0, bits ^ jnp.int32(0x7FFFFFFF), bits)


def _kth_largest(keys, k):
    t = keys[0].shape[0]

    def step(it, thr):
        bit = jnp.left_shift(jnp.int32(1), 31 - it)
        cand = thr + bit
        cnt = jnp.zeros((t, 1), F32)
        for kk in keys:
            cnt = cnt + jnp.sum(jnp.where(kk >= cand, 1.0, 0.0), axis=-1, keepdims=True)
        return jnp.where(cnt >= k, cand, thr)

    return lax.fori_loop(0, 32, step, jnp.full((t, 1), INT_MIN, jnp.int32))


def _index_scores(qi, wi, kparts):
    t = qi.shape[0]
    lane = lax.broadcasted_iota(jnp.int32, (t, LANES), 1)
    w = wi * (N_IDX_HEADS ** -0.5 * IDX_DIM ** -0.5)
    acc = [jnp.zeros((t, kp.shape[0]), F32) for kp in kparts]
    for p in range(N_IDX_HEADS // 2):
        qp = qi[:, p * LANES:(p + 1) * LANES]
        halves = (jnp.where(lane < IDX_DIM, qp, jnp.zeros_like(qp)),
                  jnp.where(lane >= IDX_DIM, qp, jnp.zeros_like(qp)))
        for s, qh in enumerate(halves):
            h = 2 * p + s
            wh = w[:, h:h + 1]
            for j, kp in enumerate(kparts):
                acc[j] = acc[j] + jnp.maximum(_dot_nt(qh, kp), 0.0) * wh
    return acc


def _softmax2(parts, shifts):
    m = None
    for tp, c in zip(parts, shifts):
        mj = tp.max(axis=-1, keepdims=True)
        if c is not None:
            mj = mj + c
        m = mj if m is None else jnp.maximum(m, mj)
    es, l = [], None
    for tp, c in zip(parts, shifts):
        e = jnp.exp2(tp - (m if c is None else m - c))
        es.append(e)
        lj = e.sum(axis=-1, keepdims=True)
        l = lj if l is None else l + lj
    return es, 1.0 / l


def _lane_half_masks(q):
    lane = lax.broadcasted_iota(jnp.int32, q.shape, 1)
    z = jnp.zeros_like(q)
    return jnp.where(lane < SUB_DIM, q, z), jnp.where(lane >= SUB_DIM, q, z)


def _lambda(lq1, lk1, lq2, lk2, lam_init):
    return (jnp.exp(jnp.sum(lq1 * lk1, axis=-1, keepdims=True))
            - jnp.exp(jnp.sum(lq2 * lk2, axis=-1, keepdims=True)) + lam_init)


def _sub_ln(o, g, lam_init):
    return o * lax.rsqrt(jnp.mean(o * o, axis=-1, keepdims=True) + RMS_EPS) * g * (1.0 - lam_init)


def _far_buckets(nq, far_max):
    lens = list(range(0, far_max, FAR_STEP)) + [far_max]
    if len(lens) > 2 and lens[-1] - lens[-2] < FAR_STEP:
        del lens[-2]
    out, lo = [], 0
    for fl in lens:
        hi = min(nq, fl // QB + 2)
        if hi > lo:
            out.append((lo, hi, fl))
            lo = hi
    assert lo == nq
    return out


def _window(ref, prev0, diag0, cols=slice(None)):
    return jnp.concatenate([ref[pl.ds(prev0, QB), cols], ref[pl.ds(diag0, QB), cols]], axis=0)


def _dsa_block(fl, topk, i, cfar_ref, qa_ref, qi_ref, wi_ref, k_ref, v_ref, kx_ref, dwin_ref,
               o_ref, mfar_ref, mwin_ref):
    prev0 = pl.multiple_of(jnp.maximum(i - 1, 0) * QB, QB)
    diag0 = pl.multiple_of(i * QB, QB)

    kparts = ([kx_ref[0:fl, :]] if fl else []) + [_window(kx_ref, prev0, diag0)]
    scores = _index_scores(qi_ref[...], wi_ref[...], kparts)
    s_win = scores[-1]
    col_win = lax.broadcasted_iota(jnp.int32, s_win.shape, 1)
    row_win = lax.broadcasted_iota(jnp.int32, s_win.shape, 0)
    ok_win = ((col_win < QB + CHUNK) | (row_win >= CHUNK)) & ((col_win >= QB) | (i > 0))
    key_win = jnp.where(ok_win, _sortable(s_win), INT_MIN)
    keys = [key_win]
    if fl:
        s_far = scores[0]
        col_far = lax.broadcasted_iota(jnp.int32, s_far.shape, 1)
        ok_far = col_far < (i - 1) * QB
        key_far = jnp.where(ok_far, _sortable(s_far), INT_MIN)
        keys = [key_far, key_win]
    thr = _kth_largest(keys, topk)
    mwin_ref[...] = jnp.where(ok_win & (key_win >= thr), 0.0, NEG)
    if fl:
        mfar_ref[:, 0:fl] = jnp.where(ok_far & (key_far >= thr), 0.0, NEG)

    k_win = _window(k_ref, prev0, diag0)
    v_win = _window(v_ref, prev0, diag0)
    c1 = HEAD_DIM ** -0.5 * LOG2E

    def head(h, carry):
        off = pl.multiple_of(h * HEAD_DIM, HEAD_DIM)
        q = qa_ref[:, pl.ds(off, HEAD_DIM)]
        tw = _dot_nt(q, k_win) * c1 + (dwin_ref[h] + mwin_ref[...])
        if fl:
            tf = _dot_nt(q, k_ref[0:fl, :]) * c1 + mfar_ref[:, 0:fl]
            (ef, ew), rl = _softmax2([tf, tw], [cfar_ref[h], None])
            o = (_dot(ef.astype(BF16), v_ref[0:fl, :]) + _dot(ew.astype(BF16), v_win)) * rl
        else:
            (ew,), rl = _softmax2([tw], [None])
            o = _dot(ew.astype(BF16), v_win) * rl
        o_ref[:, pl.ds(off, HEAD_DIM)] = o.astype(o_ref.dtype)
        return carry

    lax.fori_loop(0, N_HEADS, head, 0, unroll=HEAD_UNROLL)


def _dsa_prompt_body(buckets, topk, l_ref, *refs):
    i = pl.program_id(1)
    for lo, hi, fl in buckets:
        pl.when((i >= lo) & (i < hi))(functools.partial(_dsa_block, fl, topk, i, *refs))


def _dsa_prompt(lidx, projq, pkv_b, pkv_f, dwin, cfar, batch, seq, m_total, cols):
    nq = seq // QB
    far_max = seq - WIN
    topk = min(TOPK_MAX, seq // 4)
    width = N_HEADS * HEAD_DIM
    return pl.pallas_call(
        functools.partial(_dsa_prompt_body, _far_buckets(nq, far_max), topk),
        grid_spec=pltpu.PrefetchScalarGridSpec(
            num_scalar_prefetch=1, grid=(batch, nq),
            in_specs=[
                pl.BlockSpec(memory_space=pltpu.SMEM),
                pl.BlockSpec((QB, width), lambda b, i, l: (b * nq + i, cols["qa"])),
                pl.BlockSpec((QB, N_IDX_HEADS * IDX_DIM), lambda b, i, l: (b * nq + i, cols["qi"])),
                pl.BlockSpec((QB, LANES), lambda b, i, l: (b * nq + i, cols["wi"])),
                pl.BlockSpec((seq, HEAD_DIM), lambda b, i, l: (b, cols["ka"])),
                pl.BlockSpec((seq, HEAD_DIM), lambda b, i, l: (b, cols["va"])),
                pl.BlockSpec((seq, LANES), lambda b, i, l: (b, cols["kx"])),
                pl.BlockSpec((N_HEADS, QB, WIN), lambda b, i, l: (0, 0, 0)),
            ],
            out_specs=pl.BlockSpec((QB, width), lambda b, i, l: (b * nq + i, 0)),
            scratch_shapes=[pltpu.VMEM((QB, far_max), F32), pltpu.VMEM((QB, WIN), F32)]),
        out_shape=jax.ShapeDtypeStruct((m_total, width), BF16),
        compiler_params=_cparams("parallel", "arbitrary"),
        name="dsa_prompt",
    )(lidx, cfar, projq, projq, pkv_f, pkv_b, pkv_b, pkv_b, dwin)


def _diff_block(fl, i, l_ref, cfar_ref, laminit_ref, qb_ref, kb_ref, vb_ref, dwin_ref,
                lq1_ref, lk1_ref, lq2_ref, lk2_ref, g_ref, o_ref, mfar_ref, mwin_ref):
    prev0 = pl.multiple_of(jnp.maximum(i - 1, 0) * QB, QB)
    diag0 = pl.multiple_of(i * QB, QB)
    lam_init = laminit_ref[l_ref[0]]
    lam = _lambda(lq1_ref[...], lk1_ref[...], lq2_ref[...], lk2_ref[...], lam_init)
    col_win = lax.broadcasted_iota(jnp.int32, (QB, WIN), 1)
    mwin_ref[...] = jnp.where((col_win >= QB) | (i > 0), 0.0, NEG)
    if fl:
        col_far = lax.broadcasted_iota(jnp.int32, (QB, fl), 1)
        mfar_ref[:, 0:fl] = jnp.where(col_far < (i - 1) * QB, 0.0, NEG)
    c1 = SUB_DIM ** -0.5 * LOG2E
    g = g_ref[...]

    def head(h, carry):
        off = pl.multiple_of(h * HEAD_DIM, HEAD_DIM)
        hc = pl.ds(off, HEAD_DIM)
        q1, q2 = _lane_half_masks(qb_ref[:, hc])
        k_win = _window(kb_ref, prev0, diag0, hc)
        v_win = _window(vb_ref, prev0, diag0, hc)
        bias_win = dwin_ref[h] + mwin_ref[...]
        if fl:
            k_far = kb_ref[0:fl, hc]
            shifts = [cfar_ref[h], None]
            (e1f, e1w), r1 = _softmax2([_dot_nt(q1, k_far) * c1 + mfar_ref[:, 0:fl],
                                        _dot_nt(q1, k_win) * c1 + bias_win], shifts)
            (e2f, e2w), r2 = _softmax2([_dot_nt(q2, k_far) * c1 + mfar_ref[:, 0:fl],
                                        _dot_nt(q2, k_win) * c1 + bias_win], shifts)
            r2 = r2 * lam
            o = (_dot((e1f * r1 - e2f * r2).astype(BF16), vb_ref[0:fl, hc])
                 + _dot((e1w * r1 - e2w * r2).astype(BF16), v_win))
        else:
            (e1w,), r1 = _softmax2([_dot_nt(q1, k_win) * c1 + bias_win], [None])
            (e2w,), r2 = _softmax2([_dot_nt(q2, k_win) * c1 + bias_win], [None])
            r2 = r2 * lam
            o = _dot((e1w * r1 - e2w * r2).astype(BF16), v_win)
        o_ref[:, hc] = _sub_ln(o, g, lam_init).astype(o_ref.dtype)
        return carry

    lax.fori_loop(0, N_HEADS, head, 0, unroll=HEAD_UNROLL)


def _diff_prompt_body(buckets, l_ref, *refs):
    i = pl.program_id(1)
    for lo, hi, fl in buckets:
        pl.when((i >= lo) & (i < hi))(functools.partial(_diff_block, fl, i, l_ref, *refs))


def _diff_prompt(lidx, projq, pkv_b, dwin, cfar, lam_inits, lq1, lk1, lq2, lk2, subg, batch, seq, m_total, cols):
    nq = seq // QB
    far_max = seq - WIN
    width = N_HEADS * HEAD_DIM
    vec = lambda n: pl.BlockSpec((None, 1, n), lambda b, i, l: (l[0], 0, 0))
    return pl.pallas_call(
        functools.partial(_diff_prompt_body, _far_buckets(nq, far_max)),
        grid_spec=pltpu.PrefetchScalarGridSpec(
            num_scalar_prefetch=1, grid=(batch, nq),
            in_specs=[
                pl.BlockSpec(memory_space=pltpu.SMEM),
                pl.BlockSpec(memory_space=pltpu.SMEM),
                pl.BlockSpec((QB, width), lambda b, i, l: (b * nq + i, cols["qb"])),
                pl.BlockSpec((seq, width), lambda b, i, l: (b, cols["kb"])),
                pl.BlockSpec((seq, width), lambda b, i, l: (b, cols["vb"])),
                pl.BlockSpec((N_HEADS, QB, WIN), lambda b, i, l: (0, 0, 0)),
                vec(SUB_DIM), vec(SUB_DIM), vec(SUB_DIM), vec(SUB_DIM), vec(HEAD_DIM),
            ],
            out_specs=pl.BlockSpec((QB, width), lambda b, i, l: (b * nq + i, 0)),
            scratch_shapes=[pltpu.VMEM((QB, far_max), F32), pltpu.VMEM((QB, WIN), F32)]),
        out_shape=jax.ShapeDtypeStruct((m_total, width), BF16),
        compiler_params=_cparams("parallel", "arbitrary"),
        name="diff_prompt",
    )(lidx, cfar, lam_inits, projq, pkv_b, pkv_b, dwin, lq1, lk1, lq2, lk2, subg)


def _dsa_sample_body(topk, l_ref, qa_ref, qi_ref, wi_ref, kn_ref, vn_ref, kxn_ref, kc_ref, vc_ref, kxc_ref,
                     bc_ref, bn_ref, oin_ref, o_ref, mc_ref, mn_ref):
    s_c, s_n = _index_scores(qi_ref[...], wi_ref[...], [kxc_ref[...].astype(BF16), kxn_ref[...]])
    key_c = _sortable(s_c)
    key_n = _sortable(s_n)
    thr = _kth_largest([key_c, key_n], topk)
    mc_ref[...] = jnp.where(key_c >= thr, 0.0, NEG)
    mn_ref[...] = jnp.where(key_n >= thr, 0.0, NEG)
    k_c = kc_ref[...].astype(BF16)
    v_c = vc_ref[...].astype(BF16)
    k_n = kn_ref[...]
    v_n = vn_ref[...]
    c1 = HEAD_DIM ** -0.5 * LOG2E

    def head(h, carry):
        off = pl.multiple_of(h * HEAD_DIM, HEAD_DIM)
        q = qa_ref[:, pl.ds(off, HEAD_DIM)]
        tc = _dot_nt(q, k_c) * c1 + (bc_ref[h] + mc_ref[...])
        tn = _dot_nt(q, k_n) * c1 + (bn_ref[h] + mn_ref[...])
        (ec, en), rl = _softmax2([tc, tn], [None, None])
        o = (_dot(ec.astype(BF16), v_c) + _dot(en.astype(BF16), v_n)) * rl
        o_ref[:, pl.ds(off, HEAD_DIM)] = o.astype(o_ref.dtype)
        return carry

    lax.fori_loop(0, N_HEADS, head, 0)


def _dsa_sample(lidx, projq, pkv_b, pkv_f, cache_ka, cache_va, cache_kx2, bias_c, bias_n, o_all, row0, cols):
    _, nb, past, _ = cache_ka.shape
    t = bias_c.shape[1]
    r0 = row0 // t
    topk = min(TOPK_MAX, (past + t) // 4)
    width = N_HEADS * HEAD_DIM
    w128 = lambda key: (lambda b, l: (r0 + b, cols[key]))
    cache = lambda n: pl.BlockSpec((None, None, past, n), lambda b, l: (l[0], b, 0, 0))
    return pl.pallas_call(
        functools.partial(_dsa_sample_body, topk),
        grid_spec=pltpu.PrefetchScalarGridSpec(
            num_scalar_prefetch=1, grid=(nb,),
            in_specs=[
                pl.BlockSpec((t, width), w128("qa")),
                pl.BlockSpec((t, N_IDX_HEADS * IDX_DIM), w128("qi")),
                pl.BlockSpec((t, LANES), w128("wi")),
                pl.BlockSpec((t, HEAD_DIM), w128("ka")),
                pl.BlockSpec((t, HEAD_DIM), w128("va")),
                pl.BlockSpec((t, LANES), w128("kx")),
                cache(HEAD_DIM), cache(HEAD_DIM), cache(LANES),
                pl.BlockSpec((N_HEADS, t, past), lambda b, l: (0, 0, 0)),
                pl.BlockSpec((N_HEADS, t, t), lambda b, l: (0, 0, 0)),
                pl.BlockSpec(memory_space=pl.ANY),
            ],
            out_specs=pl.BlockSpec((t, width), lambda b, l: (r0 + b, 0)),
            scratch_shapes=[pltpu.VMEM((t, past), F32), pltpu.VMEM((t, t), F32)]),
        out_shape=jax.ShapeDtypeStruct(o_all.shape, BF16),
        input_output_aliases={12: 0},
        compiler_params=_cparams("arbitrary"),
        name="dsa_sample",
    )(lidx, projq, projq, pkv_f, pkv_b, pkv_b, pkv_b, cache_ka, cache_va, cache_kx2, bias_c, bias_n, o_all)


def _diff_sample_body(l_ref, laminit_ref, qb_ref, kn_ref, vn_ref, kc_ref, vc_ref, bc_ref, bn_ref,
                      lq1_ref, lk1_ref, lq2_ref, lk2_ref, g_ref, oin_ref, o_ref):
    lam_init = laminit_ref[l_ref[0]]
    lam = _lambda(lq1_ref[...], lk1_ref[...], lq2_ref[...], lk2_ref[...], lam_init)
    c1 = SUB_DIM ** -0.5 * LOG2E
    g = g_ref[...]

    def head(h, carry):
        off = pl.multiple_of(h * HEAD_DIM, HEAD_DIM)
        hc = pl.ds(off, HEAD_DIM)
        q1, q2 = _lane_half_masks(qb_ref[:, hc])
        k_c = kc_ref[:, hc].astype(BF16)
        v_c = vc_ref[:, hc].astype(BF16)
        k_n = kn_ref[:, hc]
        v_n = vn_ref[:, hc]
        bc = bc_ref[h]
        bn = bn_ref[h]
        none2 = [None, None]
        (e1c, e1n), r1 = _softmax2([_dot_nt(q1, k_c) * c1 + bc, _dot_nt(q1, k_n) * c1 + bn], none2)
        (e2c, e2n), r2 = _softmax2([_dot_nt(q2, k_c) * c1 + bc, _dot_nt(q2, k_n) * c1 + bn], none2)
        r2 = r2 * lam
        o = (_dot((e1c * r1 - e2c * r2).astype(BF16), v_c) + _dot((e1n * r1 - e2n * r2).astype(BF16), v_n))
        o_ref[:, hc] = _sub_ln(o, g, lam_init).astype(o_ref.dtype)
        return carry

    lax.fori_loop(0, N_HEADS, head, 0)


def _diff_sample(lidx, projq, pkv_b, cache_kb, cache_vb, bias_c, bias_n, lam_inits, lq1, lk1, lq2, lk2, subg,
                 o_all, row0, cols):
    _, nb, past, width = cache_kb.shape
    t = bias_c.shape[1]
    r0 = row0 // t
    vec = lambda n: pl.BlockSpec((None, 1, n), lambda b, l: (l[0], 0, 0))
    cache = pl.BlockSpec((None, None, past, width), lambda b, l: (l[0], b, 0, 0))
    return pl.pallas_call(
        _diff_sample_body,
        grid_spec=pltpu.PrefetchScalarGridSpec(
            num_scalar_prefetch=1, grid=(nb,),
            in_specs=[
                pl.BlockSpec(memory_space=pltpu.SMEM),
                pl.BlockSpec((t, width), lambda b, l: (r0 + b, cols["qb"])),
                pl.BlockSpec((t, width), lambda b, l: (r0 + b, cols["kb"])),
                pl.BlockSpec((t, width), lambda b, l: (r0 + b, cols["vb"])),
                cache, cache,
                pl.BlockSpec((N_HEADS, t, past), lambda b, l: (0, 0, 0)),
                pl.BlockSpec((N_HEADS, t, t), lambda b, l: (0, 0, 0)),
                vec(SUB_DIM), vec(SUB_DIM), vec(SUB_DIM), vec(SUB_DIM), vec(HEAD_DIM),
                pl.BlockSpec(memory_space=pl.ANY),
            ],
            out_specs=pl.BlockSpec((t, width), lambda b, l: (r0 + b, 0))),
        out_shape=jax.ShapeDtypeStruct(o_all.shape, BF16),
        input_output_aliases={14: 0},
        compiler_params=_cparams("arbitrary"),
        name="diff_sample",
    )(lidx, lam_inits, projq, pkv_b, pkv_b, cache_kb, cache_vb, bias_c, bias_n, lq1, lk1, lq2, lk2, subg, o_all)


def _layer_norm_rows(z, g, b):
    mu = jnp.mean(z, axis=-1, keepdims=True)
    zc = z - mu
    var = jnp.mean(zc * zc, axis=-1, keepdims=True)
    return zc * lax.rsqrt(var + LN_EPS) * g + b


def _route(rl):
    lane = lax.broadcasted_iota(jnp.int32, rl.shape, 1).astype(F32)
    big = 1e6
    gl = jnp.where(lane < N_GROUPS, rl, NEG)
    gmax = gl.max(axis=-1, keepdims=True)
    gsum = jnp.sum(jnp.exp(gl - gmax), axis=-1, keepdims=True)
    g_val = 1.0 / gsum
    g_idx = jnp.min(jnp.where(gl == gmax, lane, big), axis=-1, keepdims=True)
    lo = N_GROUPS + g_idx * EXPERTS_PER_GROUP
    el = jnp.where((lane >= lo) & (lane < lo + EXPERTS_PER_GROUP), rl, NEG)
    m1 = el.max(axis=-1, keepdims=True)
    i1 = jnp.min(jnp.where(el == m1, lane, big), axis=-1, keepdims=True)
    el2 = jnp.where(lane == i1, NEG, el)
    m2 = el2.max(axis=-1, keepdims=True)
    i2 = jnp.min(jnp.where(el2 == m2, lane, big), axis=-1, keepdims=True)
    e2 = jnp.exp(m2 - m1)
    den = 1.0 / (1.0 + e2)
    w1 = den * g_val
    w2 = e2 * den * g_val
    out = jnp.where(lane == 0, i1 - N_GROUPS, 0.0)
    out = jnp.where(lane == 1, i2 - N_GROUPS, out)
    out = jnp.where(lane == 2, w1, out)
    out = jnp.where(lane == 3, w2, out)
    return out


HI16 = -65536


def _pack_bf16_pair(x):
    n = x.shape[1] // 2
    bits = pltpu.bitcast(x.astype(BF16).astype(F32), jnp.int32)
    return lax.shift_right_logical(bits[:, :n], 16) | (bits[:, n:] & HI16)


def _unpack_bf16_pair(p):
    return pltpu.bitcast(lax.shift_left(p, 16), F32), pltpu.bitcast(p & HI16, F32)


def _out_ln_body(alpha, tn, rows, l_ref, oa_ref, ob_ref, wa_ref, wb_ref, x_ref, g_ref, b_ref, wr_ref,
                 h_ref, hp_ref, rt_ref):
    j = pl.program_id(1)
    acc = _dot(oa_ref[...], wa_ref[...]) + _dot(ob_ref[...], wb_ref[...])
    off = pl.multiple_of(j * tn, tn)
    h_ref[:, pl.ds(off, tn)] = alpha * x_ref[...] + acc

    @pl.when(j == pl.num_programs(1) - 1)
    def _():
        g = g_ref[...]
        b = b_ref[...]

        def chunk(c, carry):
            r = pl.multiple_of(c * rows, rows)
            hn = _layer_norm_rows(h_ref[pl.ds(r, rows), :], g, b)
            h_ref[pl.ds(r, rows), :] = hn
            hp_ref[pl.ds(r, rows), :] = _pack_bf16_pair(hn)
            rt_ref[pl.ds(r, rows), :] = _dot(hn.astype(BF16), wr_ref[...])
            return carry

        lax.fori_loop(0, h_ref.shape[0] // rows, chunk, 0)
        rt_ref[...] = _route(rt_ref[...])


def _out_ln(lidx, oa, ob, w_out, x, g, b, wr, alpha, tm, tn):
    m, half = oa.shape
    d = w_out.shape[-1]
    rows = next(r for r in (64, 48, 32, 16, 8) if tm % r == 0)
    return pl.pallas_call(
        functools.partial(_out_ln_body, alpha, tn, rows),
        grid_spec=pltpu.PrefetchScalarGridSpec(
            num_scalar_prefetch=1, grid=(m // tm, d // tn),
            in_specs=[
                pl.BlockSpec((tm, half), lambda i, j, l: (i, 0)),
                pl.BlockSpec((tm, half), lambda i, j, l: (i, 0)),
                pl.BlockSpec((None, half, tn), lambda i, j, l: (l[0], 0, j)),
                pl.BlockSpec((None, half, tn), lambda i, j, l: (l[0], 1, j)),
                pl.BlockSpec((tm, tn), lambda i, j, l: (i, j)),
                pl.BlockSpec((None, 1, d), lambda i, j, l: (l[0], 0, 0)),
                pl.BlockSpec((None, 1, d), lambda i, j, l: (l[0], 0, 0)),
                pl.BlockSpec((None, d, LANES), lambda i, j, l: (l[0], 0, 0)),
            ],
            out_specs=[pl.BlockSpec((tm, d), lambda i, j, l: (i, 0)),
                       pl.BlockSpec((tm, d // 2), lambda i, j, l: (i, 0)),
                       pl.BlockSpec((tm, LANES), lambda i, j, l: (i, 0))]),
        out_shape=[jax.ShapeDtypeStruct((m, d), F32), jax.ShapeDtypeStruct((m, d // 2), jnp.int32),
                   jax.ShapeDtypeStruct((m, LANES), F32)],
        compiler_params=_cparams("parallel", "arbitrary"),
        name="out_proj_ln_route",
    )(lidx, oa, ob, w_out, w_out, x, g, b, wr)


def _moe_body(tm, spare_row, meta_ref, te_ref, src_ref, nxt_ref, dst_ref, prv_ref, gate_ref, h_hbm,
              w1_ref, w3_ref, w2_ref, y_hbm, xbuf, obuf, xb, gsem, ssem):
    t = pl.program_id(0)
    n_used = meta_ref[1]
    slot = lax.rem(t, 2)
    other = 1 - slot

    def wait_gather(s):
        pltpu.make_async_copy(h_hbm.at[pl.ds(0, tm)], xbuf.at[s], gsem.at[s]).wait()

    def wait_scatter(s):
        pltpu.make_async_copy(obuf.at[s], y_hbm.at[pl.ds(0, tm)], ssem.at[s]).wait()

    @pl.when(t == 0)
    def _():
        def row(r, carry):
            pltpu.make_async_copy(h_hbm.at[pl.ds(src_ref[0, r], 1)], xbuf.at[0, pl.ds(r, 1)], gsem.at[0]).start()
            return carry
        lax.fori_loop(0, tm, row, 0, unroll=8)
        obuf[...] = jnp.zeros_like(obuf)
        pltpu.make_async_copy(obuf.at[0], y_hbm.at[pl.ds(spare_row, tm)], ssem.at[0]).start()

    @pl.when(t < n_used)
    def _():
        wait_gather(slot)
        wait_scatter(slot)
        lo, hi = _unpack_bf16_pair(xbuf[slot])
        half = lo.shape[1]
        xb[:, 0:half] = lo.astype(BF16)
        xb[:, half:2 * half] = hi.astype(BF16)
        for r in range(tm):
            pltpu.make_async_copy(h_hbm.at[pl.ds(nxt_ref[0, r], 1)], xbuf.at[other, pl.ds(r, 1)],
                                  gsem.at[other]).start()
        for r in range(tm):
            pltpu.make_async_copy(obuf.at[other, pl.ds(r, 1)], y_hbm.at[pl.ds(prv_ref[0, r], 1)],
                                  ssem.at[other]).start()
        x = xb[...]
        u = _dot(x, w1_ref[...])
        v = _dot(x, w3_ref[...])
        a = (u * (1.0 / (1.0 + jnp.exp(-u))) * v).astype(BF16)
        obuf[slot] = _pack_bf16_pair(_dot(a, w2_ref[...]) * gate_ref[:, 0:1])

        @pl.when(t + 1 >= n_used)
        def _():
            def row(r, carry):
                pltpu.make_async_copy(obuf.at[slot, pl.ds(r, 1)], y_hbm.at[pl.ds(dst_ref[0, r], 1)],
                                      ssem.at[slot]).start()
                return carry
            lax.fori_loop(0, tm, row, 0, unroll=8)
            wait_gather(other)
            wait_scatter(other)
            wait_scatter(slot)


def _moe(meta, tile_expert, src3, dst3, prv3, gate, hp, w1, w3, w2, tm, y_rows):
    d, f = w1.shape[-2:]
    dp = hp.shape[1]
    nt = src3.shape[0]
    spare_row = y_rows - tm
    idx_spec = lambda imap: pl.BlockSpec((None, 1, tm), imap, memory_space=pltpu.SMEM)
    wspec = lambda shape: pl.BlockSpec((None, None) + shape, lambda t, meta, te: (meta[0], te[t], 0, 0),
                                       pipeline_mode=pl.Buffered(1))
    return pl.pallas_call(
        functools.partial(_moe_body, tm, spare_row),
        grid_spec=pltpu.PrefetchScalarGridSpec(
            num_scalar_prefetch=2, grid=(nt,),
            in_specs=[
                idx_spec(lambda t, meta, te: (t, 0, 0)),
                idx_spec(lambda t, meta, te: (jnp.minimum(t + 1, nt - 1), 0, 0)),
                idx_spec(lambda t, meta, te: (t, 0, 0)),
                idx_spec(lambda t, meta, te: (t, 0, 0)),
                pl.BlockSpec((tm, LANES), lambda t, meta, te: (t, 0)),
                pl.BlockSpec(memory_space=pl.ANY),
                wspec((d, f)), wspec((d, f)), wspec((f, d)),
            ],
            out_specs=pl.BlockSpec(memory_space=pl.ANY),
            scratch_shapes=[pltpu.VMEM((2, tm, dp), jnp.int32), pltpu.VMEM((2, tm, dp), jnp.int32),
                            pltpu.VMEM((tm, d), BF16),
                            pltpu.SemaphoreType.DMA((2,)), pltpu.SemaphoreType.DMA((2,))]),
        out_shape=jax.ShapeDtypeStruct((y_rows, dp), jnp.int32),
        compiler_params=_cparams("arbitrary"),
        name="moe_experts",
    )(meta, tile_expert, src3, src3, dst3, prv3, gate, hp, w1, w3, w2)


def _dispatch_plan(route, tm):
    m = route.shape[0]
    npairs = 2 * m
    nt = npairs // tm + N_EXPERTS
    p = nt * tm
    e = route[:, 0:2].astype(jnp.int32).reshape(npairs)
    wgt = route[:, 2:4].reshape(npairs)
    order = jnp.argsort(e, stable=True).astype(jnp.int32)
    counts = jnp.sum((e[:, None] == jnp.arange(N_EXPERTS, dtype=jnp.int32)[None, :]).astype(jnp.int32), axis=0)
    padded = ((counts + tm - 1) // tm) * tm
    pad_end = jnp.cumsum(padded)
    pad_start = pad_end - padded
    cnt_start = jnp.cumsum(counts) - counts
    tile_start = jnp.arange(nt, dtype=jnp.int32) * tm
    tile_expert = jnp.minimum(jnp.sum((pad_end[None, :] <= tile_start[:, None]).astype(jnp.int32), axis=1),
                              N_EXPERTS - 1)
    slot = jnp.arange(p, dtype=jnp.int32)
    se = jnp.repeat(tile_expert, tm)
    rank = slot - pad_start[se]
    valid = (rank >= 0) & (rank < counts[se])
    pair = order[jnp.clip(cnt_start[se] + rank, 0, npairs - 1)]
    src = jnp.where(valid, pair // 2, 0)
    dst = jnp.where(valid, (pair % 2) * m + pair // 2, npairs + slot)
    gate = jnp.where(valid, wgt[pair], 0.0)
    n_used = (pad_end[-1] // tm).astype(jnp.int32)
    gate128 = jnp.broadcast_to(gate[:, None], (p, LANES))
    dst3 = dst.reshape(nt, 1, tm)
    spare = (npairs + p + jnp.arange(tm, dtype=jnp.int32)).reshape(1, 1, tm)
    prv3 = jnp.concatenate([spare, dst3[:-1]], axis=0)
    return src.reshape(nt, 1, tm), dst3, prv3, gate128, tile_expert.astype(jnp.int32), n_used


def _ln2_body(alpha, l_ref, h_ref, y0_ref, y1_ref, g_ref, b_ref, o_ref, ob_ref):
    lo0, hi0 = _unpack_bf16_pair(y0_ref[...])
    lo1, hi1 = _unpack_bf16_pair(y1_ref[...])
    z = alpha * h_ref[...] + jnp.concatenate([lo0 + lo1, hi0 + hi1], axis=1)
    hn = _layer_norm_rows(z, g_ref[...], b_ref[...])
    o_ref[...] = hn
    ob_ref[...] = hn.astype(BF16)


def _ln2(lidx, h, y, g, b, alpha, tm):
    m, d = h.shape
    nblk = m // tm
    return pl.pallas_call(
        functools.partial(_ln2_body, alpha),
        grid_spec=pltpu.PrefetchScalarGridSpec(
            num_scalar_prefetch=1, grid=(nblk,),
            in_specs=[
                pl.BlockSpec((tm, d), lambda i, l: (i, 0)),
                pl.BlockSpec((tm, d // 2), lambda i, l: (i, 0)),
                pl.BlockSpec((tm, d // 2), lambda i, l: (i + nblk, 0)),
                pl.BlockSpec((None, 1, d), lambda i, l: (l[0], 0, 0)),
                pl.BlockSpec((None, 1, d), lambda i, l: (l[0], 0, 0)),
            ],
            out_specs=[pl.BlockSpec((tm, d), lambda i, l: (i, 0)),
                       pl.BlockSpec((tm, d), lambda i, l: (i, 0))]),
        out_shape=[jax.ShapeDtypeStruct((m, d), F32), jax.ShapeDtypeStruct((m, d), BF16)],
        compiler_params=_cparams("parallel"),
        name="moe_combine_ln",
    )(lidx, h, y, y, g, b)


def _pick_tile(m, candidates):
    for c in candidates:
        if m % c == 0:
            return c
    raise ValueError(f"no row tile for {m} rows")


def kernel(x_prompt, x_sample, cache_ka, cache_va, cache_kidx, cache_kb, cache_vb, rel_bias, w_in, w_out,
           lambda_q1, lambda_k1, lambda_q2, lambda_k2, subln_g, ln1_g, ln1_b, ln2_g, ln2_b,
           w_route_group, w_route_expert, w1, w3, w2):
    batch, seq, d = x_prompt.shape
    dec_batch, dec_seq, _ = x_sample.shape
    depth = w_in.shape[0]
    past = cache_ka.shape[2]
    mp = batch * seq
    ms = dec_batch * dec_seq
    m = mp + ms
    alpha = (2.0 * depth) ** 0.25
    width = N_HEADS * HEAD_DIM
    assert seq % QB == 0 and seq >= 2 * WIN and mp % dec_seq == 0 and width == 2048

    in_widths = (width, HEAD_DIM, HEAD_DIM, N_IDX_HEADS * IDX_DIM, IDX_DIM, N_IDX_HEADS, width, width, width)
    splits = [int(s) for s in np.cumsum(in_widths)[:-1]]
    wqa, wka, wva, wqi, wki, wwi, wqb, wkb, wvb = jnp.split(w_in.astype(BF16), splits, axis=-1)
    w_q = jnp.concatenate([wqa, wqb, wqi], axis=-1)
    pad = jnp.zeros(wwi.shape[:-1] + (LANES - N_IDX_HEADS,), BF16)
    w_kv = jnp.concatenate([wkb, wvb, wka, wva, wki, wki, wwi, pad], axis=-1)
    n_kv = w_kv.shape[-1]
    cols_q = {"qa": 0, "qb": 1, "qi": 4}
    cols_kv = {"kb": 0, "vb": 1, "ka": 32, "va": 33, "kx": 34, "wi": 35}
    cols = {**cols_q, **cols_kv}
    w_out_b = w_out.astype(BF16)
    w1_b = w1.astype(BF16)
    w3_b = w3.astype(BF16)
    w2_b = w2.astype(BF16)
    w_r = jnp.concatenate([w_route_group, w_route_expert,
                           jnp.zeros((depth, d, LANES - N_GROUPS - N_EXPERTS), F32)], axis=-1).astype(BF16)
    vec3 = lambda a: a.reshape(depth, 1, a.shape[-1])
    lq1, lk1, lq2, lk2, subg = map(vec3, (lambda_q1, lambda_k1, lambda_q2, lambda_k2, subln_g))
    g1, b1, g2, b2 = map(vec3, (ln1_g, ln1_b, ln2_g, ln2_b))
    lam_inits = jnp.asarray([0.8 - 0.6 * math.exp(-0.3 * l) for l in range(depth)], F32)

    dwin, cfar = _prompt_bias_tiles(rel_bias)
    sbias = _sample_bias(rel_bias, past, dec_seq)
    dwin_a, dwin_b = dwin[:N_HEADS], dwin[N_HEADS:]
    cfar_a, cfar_b = cfar[:N_HEADS], cfar[N_HEADS:]
    sb_a_c, sb_a_n = sbias[:N_HEADS, :, :past], sbias[:N_HEADS, :, past:]
    sb_b_c, sb_b_n = sbias[N_HEADS:, :, :past], sbias[N_HEADS:, :, past:]
    cache_kb2 = cache_kb.astype(BF16).reshape(cache_kb.shape[:3] + (width,))
    cache_vb2 = cache_vb.astype(BF16).reshape(cache_vb.shape[:3] + (width,))
    cache_kx2 = jnp.concatenate([cache_kidx, cache_kidx], axis=-1)

    tm_proj = _pick_tile(m, (768, 512, 256, 128, 64, 32))
    tm_out = _pick_tile(m, (528, 384, 256, 128, 64, 32))
    tm_ln2 = _pick_tile(m, (256, 128, 64, 32))
    tm_moe = 256 if m >= 1024 else 32
    y_rows = 2 * m + (2 * m // tm_moe + N_EXPERTS) * tm_moe + 2 * tm_moe

    x0 = jnp.concatenate([x_prompt.reshape(mp, d), x_sample.reshape(ms, d)], axis=0)
    x, xb = x0, x0.astype(BF16)
    rows = []
    for layer in range(depth):
        lidx = jnp.full((1,), layer, jnp.int32)
        projq = _proj(lidx, xb, w_q, tm_proj, 512, False)[0]
        pkv_b, pkv_f = _proj(lidx, xb, w_kv, tm_proj, 512, True)
        oa = _dsa_prompt(lidx, projq, pkv_b, pkv_f, dwin_a, cfar_a, batch, seq, m, cols)
        ob = _diff_prompt(lidx, projq, pkv_b, dwin_b, cfar_b, lam_inits, lq1, lk1, lq2, lk2, subg,
                          batch, seq, m, cols)
        oa = _dsa_sample(lidx, projq, pkv_b, pkv_f, cache_ka, cache_va, cache_kx2, sb_a_c, sb_a_n, oa, mp, cols)
        ob = _diff_sample(lidx, projq, pkv_b, cache_kb2, cache_vb2, sb_b_c, sb_b_n, lam_inits,
                          lq1, lk1, lq2, lk2, subg, ob, mp, cols)
        h1, h1p, route = _out_ln(lidx, oa, ob, w_out_b, x, g1, b1, w_r, alpha, tm_out, 512)
        src3, dst3, prv3, gate128, tile_expert, n_used = _dispatch_plan(route, tm_moe)
        meta = jnp.stack([lidx[0], n_used])
        y = _moe(meta, tile_expert, src3, dst3, prv3, gate128, h1p, w1_b, w3_b, w2_b, tm_moe, y_rows)
        x, xb = _ln2(lidx, h1, y, g2, b2, alpha, tm_ln2)
        rows.append(pkv_f)
    y = x
    heads = (N_HEADS, 2 * SUB_DIM)

    def split(c0, n, tail):
        return (jnp.stack([r[:mp, c0:c0 + n] for r in rows]).reshape((depth, batch, seq) + tail),
                jnp.stack([r[mp:, c0:c0 + n] for r in rows]).reshape((depth, dec_batch, dec_seq) + tail))

    p_kb, s_kb = split(0, width, heads)
    p_vb, s_vb = split(width, width, heads)
    p_ka, s_ka = split(2 * width, HEAD_DIM, (HEAD_DIM,))
    p_va, s_va = split(2 * width + HEAD_DIM, HEAD_DIM, (HEAD_DIM,))
    p_kx, s_kx = split(2 * width + 2 * HEAD_DIM, IDX_DIM, (IDX_DIM,))
    y_prompt = y[:mp].reshape(batch, seq, d)
    y_sample = y[mp:].reshape(dec_batch, dec_seq, d)
    return (y_prompt, y_sample, p_ka, p_va, p_kx, p_kb, p_vb, s_ka, s_va, s_kx, s_kb, s_vb)
```

```python
import functools
import math

import numpy as np
import jax
import jax.numpy as jnp
from jax import lax
from jax.experimental import pallas as pl
from jax.experimental.pallas import tpu as pltpu

CHUNK = 64
HEAD_DIM = 128
N_HEADS = 16
N_IDX_HEADS = 16
IDX_DIM = 64
SUB_DIM = 64
TOPK_MAX = 256
NUM_BUCKETS = 32
MAX_DISTANCE = 128
N_GROUPS = 4
EXPERTS_PER_GROUP = 4
N_EXPERTS = 16
LN_EPS = 1e-5
RMS_EPS = 1e-5

LANES = 128
QB = 128
WIN = 2 * QB
FAR_STEP = 512
HEAD_UNROLL = 4
NEG = -1e30
INT_MIN = -2147483648
LOG2E = 1.4426950408889634
VMEM_LIMIT = 56 * 1024 * 1024

F32 = jnp.float32
BF16 = jnp.bfloat16


def _cparams(*sem):
    return pltpu.CompilerParams(dimension_semantics=sem, vmem_limit_bytes=VMEM_LIMIT)


def _dot_nt(a, b):
    return lax.dot_general(a, b, (((1,), (1,)), ((), ())), preferred_element_type=F32)


def _dot(a, b):
    return jnp.dot(a, b, preferred_element_type=F32)


def _bucket_np(rel):
    nb = NUM_BUCKETS // 2
    ret = np.where(rel > 0, nb, 0)
    n = np.abs(rel)
    max_exact = nb // 2
    nf = np.maximum(n, 1).astype(np.float32)
    frac = np.log(nf / np.float32(max_exact)) / np.float32(math.log(MAX_DISTANCE / max_exact))
    large = max_exact + (frac.astype(np.float32) * np.float32(nb - max_exact)).astype(np.int32)
    large = np.minimum(large, nb - 1)
    return (ret + np.where(n < max_exact, n, large)).astype(np.int32)


def _bias_lookup(rel_bias, bucket):
    onehot = (jnp.asarray(bucket.reshape(-1, 1)) == jnp.arange(NUM_BUCKETS, dtype=jnp.int32)[None, :]).astype(F32)
    vals = jnp.dot(onehot, rel_bias.astype(F32), precision=lax.Precision.HIGHEST)
    return jnp.transpose(vals).reshape((rel_bias.shape[1],) + bucket.shape)


def _prompt_bias_tiles(rel_bias):
    ql = np.arange(QB)[:, None]
    c = np.arange(WIN)[None, :]
    rel = c - QB - ql
    kl = c - QB
    adm = (c < QB) | ((kl // CHUNK) <= (ql // CHUNK))
    tiles = _bias_lookup(rel_bias, _bucket_np(rel)) * LOG2E
    tiles = jnp.where(jnp.asarray(adm)[None], tiles, NEG)
    far = rel_bias[int(_bucket_np(np.array([-(QB + 1)]))[0])] * LOG2E
    return tiles.astype(F32), far.astype(F32)


def _sample_bias(rel_bias, past_len, dec_seq):
    q = past_len + np.arange(dec_seq)[:, None]
    k = np.arange(past_len + dec_seq)[None, :]
    return (_bias_lookup(rel_bias, _bucket_np(k - q)) * LOG2E).astype(F32)


def _proj_body(l_ref, x_ref, w_ref, ob_ref, *of_ref):
    acc = _dot(x_ref[...], w_ref[...])
    ob_ref[...] = acc.astype(BF16)
    if of_ref:
        of_ref[0][...] = acc


def _proj(lidx, xb, w, tm, tn, with_f32):
    m, d = xb.shape
    n = w.shape[-1]
    out_shape = [jax.ShapeDtypeStruct((m, n), BF16)]
    out_specs = [pl.BlockSpec((tm, tn), lambda i, j, l: (i, j))]
    if with_f32:
        out_shape.append(jax.ShapeDtypeStruct((m, n), F32))
        out_specs.append(pl.BlockSpec((tm, tn), lambda i, j, l: (i, j)))
    return pl.pallas_call(
        _proj_body,
        grid_spec=pltpu.PrefetchScalarGridSpec(
            num_scalar_prefetch=1, grid=(m // tm, n // tn),
            in_specs=[pl.BlockSpec((tm, d), lambda i, j, l: (i, 0)),
                      pl.BlockSpec((None, d, tn), lambda i, j, l: (l[0], 0, j))],
            out_specs=out_specs),
        out_shape=out_shape,
        compiler_params=_cparams("parallel", "arbitrary"),
        name="in_proj_rows" if with_f32 else "in_proj",
    )(lidx, xb, w)


def _sortable(score):
    bits = pltpu.bitcast(score + 0.0, jnp.int32)
    return jnp.where(bits < 0, bits ^ jnp.int32(0x7FFFFFFF), bits)


def _kth_largest(keys, k):
    t = keys[0].shape[0]

    def step(it, thr):
        bit = jnp.left_shift(jnp.int32(1), 31 - it)
        cand = thr + bit
        cnt = jnp.zeros((t, 1), F32)
        for kk in keys:
            cnt = cnt + jnp.sum(jnp.where(kk >= cand, 1.0, 0.0), axis=-1, keepdims=True)
        return jnp.where(cnt >= k, cand, thr)

    return lax.fori_loop(0, 32, step, jnp.full((t, 1), INT_MIN, jnp.int32))


def _index_scores(qi, wi, kparts):
    t = qi.shape[0]
    lane = lax.broadcasted_iota(jnp.int32, (t, LANES), 1)
    w = wi * (N_IDX_HEADS ** -0.5 * IDX_DIM ** -0.5)
    acc = [jnp.zeros((t, kp.shape[0]), F32) for kp in kparts]
    for p in range(N_IDX_HEADS // 2):
        qp = qi[:, p * LANES:(p + 1) * LANES]
        halves = (jnp.where(lane < IDX_DIM, qp, jnp.zeros_like(qp)),
                  jnp.where(lane >= IDX_DIM, qp, jnp.zeros_like(qp)))
        for s, qh in enumerate(halves):
            h = 2 * p + s
            wh = w[:, h:h + 1]
            for j, kp in enumerate(kparts):
                acc[j] = acc[j] + jnp.maximum(_dot_nt(qh, kp), 0.0) * wh
    return acc


def _softmax2(parts, shifts):
    m = None
    for tp, c in zip(parts, shifts):
        mj = tp.max(axis=-1, keepdims=True)
        if c is not None:
            mj = mj + c
        m = mj if m is None else jnp.maximum(m, mj)
    es, l = [], None
    for tp, c in zip(parts, shifts):
        e = jnp.exp2(tp - (m if c is None else m - c))
        es.append(e)
        lj = e.sum(axis=-1, keepdims=True)
        l = lj if l is None else l + lj
    return es, 1.0 / l


def _lane_half_masks(q):
    lane = lax.broadcasted_iota(jnp.int32, q.shape, 1)
    z = jnp.zeros_like(q)
    return jnp.where(lane < SUB_DIM, q, z), jnp.where(lane >= SUB_DIM, q, z)


def _lambda(lq1, lk1, lq2, lk2, lam_init):
    return (jnp.exp(jnp.sum(lq1 * lk1, axis=-1, keepdims=True))
            - jnp.exp(jnp.sum(lq2 * lk2, axis=-1, keepdims=True)) + lam_init)


def _sub_ln(o, g, lam_init):
    return o * lax.rsqrt(jnp.mean(o * o, axis=-1, keepdims=True) + RMS_EPS) * g * (1.0 - lam_init)


def _far_buckets(nq, far_max):
    lens = list(range(0, far_max, FAR_STEP)) + [far_max]
    if len(lens) > 2 and lens[-1] - lens[-2] < FAR_STEP:
        del lens[-2]
    out, lo = [], 0
    for fl in lens:
        hi = min(nq, fl // QB + 2)
        if hi > lo:
            out.append((lo, hi, fl))
            lo = hi
    assert lo == nq
    return out


def _window(ref, prev0, diag0, cols=slice(None)):
    return jnp.concatenate([ref[pl.ds(prev0, QB), cols], ref[pl.ds(diag0, QB), cols]], axis=0)


def _dsa_block(fl, topk, i, cfar_ref, qa_ref, qi_ref, wi_ref, k_ref, v_ref, kx_ref, dwin_ref,
               o_ref, mfar_ref, mwin_ref):
    prev0 = pl.multiple_of(jnp.maximum(i - 1, 0) * QB, QB)
    diag0 = pl.multiple_of(i * QB, QB)

    kparts = ([kx_ref[0:fl, :]] if fl else []) + [_window(kx_ref, prev0, diag0)]
    scores = _index_scores(qi_ref[...], wi_ref[...], kparts)
    s_win = scores[-1]
    col_win = lax.broadcasted_iota(jnp.int32, s_win.shape, 1)
    row_win = lax.broadcasted_iota(jnp.int32, s_win.shape, 0)
    ok_win = ((col_win < QB + CHUNK) | (row_win >= CHUNK)) & ((col_win >= QB) | (i > 0))
    key_win = jnp.where(ok_win, _sortable(s_win), INT_MIN)
    keys = [key_win]
    if fl:
        s_far = scores[0]
        col_far = lax.broadcasted_iota(jnp.int32, s_far.shape, 1)
        ok_far = col_far < (i - 1) * QB
        key_far = jnp.where(ok_far, _sortable(s_far), INT_MIN)
        keys = [key_far, key_win]
    thr = _kth_largest(keys, topk)
    mwin_ref[...] = jnp.where(ok_win & (key_win >= thr), 0.0, NEG)
    if fl:
        mfar_ref[:, 0:fl] = jnp.where(ok_far & (key_far >= thr), 0.0, NEG)

    k_win = _window(k_ref, prev0, diag0)
    v_win = _window(v_ref, prev0, diag0)
    c1 = HEAD_DIM ** -0.5 * LOG2E

    def head(h, carry):
        off = pl.multiple_of(h * HEAD_DIM, HEAD_DIM)
        q = qa_ref[:, pl.ds(off, HEAD_DIM)]
        tw = _dot_nt(q, k_win) * c1 + (dwin_ref[h] + mwin_ref[...])
        if fl:
            tf = _dot_nt(q, k_ref[0:fl, :]) * c1 + mfar_ref[:, 0:fl]
            (ef, ew), rl = _softmax2([tf, tw], [cfar_ref[h], None])
            o = (_dot(ef.astype(BF16), v_ref[0:fl, :]) + _dot(ew.astype(BF16), v_win)) * rl
        else:
            (ew,), rl = _softmax2([tw], [None])
            o = _dot(ew.astype(BF16), v_win) * rl
        o_ref[:, pl.ds(off, HEAD_DIM)] = o.astype(o_ref.dtype)
        return carry

    lax.fori_loop(0, N_HEADS, head, 0, unroll=HEAD_UNROLL)


def _dsa_prompt_body(buckets, topk, l_ref, *refs):
    i = pl.program_id(1)
    for lo, hi, fl in buckets:
        pl.when((i >= lo) & (i < hi))(functools.partial(_dsa_block, fl, topk, i, *refs))


def _dsa_prompt(lidx, projq, pkv_b, pkv_f, dwin, cfar, batch, seq, m_total, cols):
    nq = seq // QB
    far_max = seq - WIN
    topk = min(TOPK_MAX, seq // 4)
    width = N_HEADS * HEAD_DIM
    return pl.pallas_call(
        functools.partial(_dsa_prompt_body, _far_buckets(nq, far_max), topk),
        grid_spec=pltpu.PrefetchScalarGridSpec(
            num_scalar_prefetch=1, grid=(batch, nq),
            in_specs=[
                pl.BlockSpec(memory_space=pltpu.SMEM),
                pl.BlockSpec((QB, width), lambda b, i, l: (b * nq + i, cols["qa"])),
                pl.BlockSpec((QB, N_IDX_HEADS * IDX_DIM), lambda b, i, l: (b * nq + i, cols["qi"])),
                pl.BlockSpec((QB, LANES), lambda b, i, l: (b * nq + i, cols["wi"])),
                pl.BlockSpec((seq, HEAD_DIM), lambda b, i, l: (b, cols["ka"])),
                pl.BlockSpec((seq, HEAD_DIM), lambda b, i, l: (b, cols["va"])),
                pl.BlockSpec((seq, LANES), lambda b, i, l: (b, cols["kx"])),
                pl.BlockSpec((N_HEADS, QB, WIN), lambda b, i, l: (0, 0, 0)),
            ],
            out_specs=pl.BlockSpec((QB, width), lambda b, i, l: (b * nq + i, 0)),
            scratch_shapes=[pltpu.VMEM((QB, far_max), F32), pltpu.VMEM((QB, WIN), F32)]),
        out_shape=jax.ShapeDtypeStruct((m_total, width), BF16),
        compiler_params=_cparams("parallel", "arbitrary"),
        name="dsa_prompt",
    )(lidx, cfar, projq, projq, pkv_f, pkv_b, pkv_b, pkv_b, dwin)


def _diff_block(fl, i, l_ref, cfar_ref, laminit_ref, qb_ref, kb_ref, vb_ref, dwin_ref,
                lq1_ref, lk1_ref, lq2_ref, lk2_ref, g_ref, o_ref, mfar_ref, mwin_ref):
    prev0 = pl.multiple_of(jnp.maximum(i - 1, 0) * QB, QB)
    diag0 = pl.multiple_of(i * QB, QB)
    lam_init = laminit_ref[l_ref[0]]
    lam = _lambda(lq1_ref[...], lk1_ref[...], lq2_ref[...], lk2_ref[...], lam_init)
    col_win = lax.broadcasted_iota(jnp.int32, (QB, WIN), 1)
    mwin_ref[...] = jnp.where((col_win >= QB) | (i > 0), 0.0, NEG)
    if fl:
        col_far = lax.broadcasted_iota(jnp.int32, (QB, fl), 1)
        mfar_ref[:, 0:fl] = jnp.where(col_far < (i - 1) * QB, 0.0, NEG)
    c1 = SUB_DIM ** -0.5 * LOG2E
    g = g_ref[...]

    def head(h, carry):
        off = pl.multiple_of(h * HEAD_DIM, HEAD_DIM)
        hc = pl.ds(off, HEAD_DIM)
        q1, q2 = _lane_half_masks(qb_ref[:, hc])
        k_win = _window(kb_ref, prev0, diag0, hc)
        v_win = _window(vb_ref, prev0, diag0, hc)
        bias_win = dwin_ref[h] + mwin_ref[...]
        if fl:
            k_far = kb_ref[0:fl, hc]
            shifts = [cfar_ref[h], None]
            (e1f, e1w), r1 = _softmax2([_dot_nt(q1, k_far) * c1 + mfar_ref[:, 0:fl],
                                        _dot_nt(q1, k_win) * c1 + bias_win], shifts)
            (e2f, e2w), r2 = _softmax2([_dot_nt(q2, k_far) * c1 + mfar_ref[:, 0:fl],
                                        _dot_nt(q2, k_win) * c1 + bias_win], shifts)
            r2 = r2 * lam
            o = (_dot((e1f * r1 - e2f * r2).astype(BF16), vb_ref[0:fl, hc])
                 + _dot((e1w * r1 - e2w * r2).astype(BF16), v_win))
        else:
            (e1w,), r1 = _softmax2([_dot_nt(q1, k_win) * c1 + bias_win], [None])
            (e2w,), r2 = _softmax2([_dot_nt(q2, k_win) * c1 + bias_win], [None])
            r2 = r2 * lam
            o = _dot((e1w * r1 - e2w * r2).astype(BF16), v_win)
        o_ref[:, hc] = _sub_ln(o, g, lam_init).astype(o_ref.dtype)
        return carry

    lax.fori_loop(0, N_HEADS, head, 0, unroll=HEAD_UNROLL)


def _diff_prompt_body(buckets, l_ref, *refs):
    i = pl.program_id(1)
    for lo, hi, fl in buckets:
        pl.when((i >= lo) & (i < hi))(functools.partial(_diff_block, fl, i, l_ref, *refs))


def _diff_prompt(lidx, projq, pkv_b, dwin, cfar, lam_inits, lq1, lk1, lq2, lk2, subg, batch, seq, m_total, cols):
    nq = seq // QB
    far_max = seq - WIN
    width = N_HEADS * HEAD_DIM
    vec = lambda n: pl.BlockSpec((None, 1, n), lambda b, i, l: (l[0], 0, 0))
    return pl.pallas_call(
        functools.partial(_diff_prompt_body, _far_buckets(nq, far_max)),
        grid_spec=pltpu.PrefetchScalarGridSpec(
            num_scalar_prefetch=1, grid=(batch, nq),
            in_specs=[
                pl.BlockSpec(memory_space=pltpu.SMEM),
                pl.BlockSpec(memory_space=pltpu.SMEM),
                pl.BlockSpec((QB, width), lambda b, i, l: (b * nq + i, cols["qb"])),
                pl.BlockSpec((seq, width), lambda b, i, l: (b, cols["kb"])),
                pl.BlockSpec((seq, width), lambda b, i, l: (b, cols["vb"])),
                pl.BlockSpec((N_HEADS, QB, WIN), lambda b, i, l: (0, 0, 0)),
                vec(SUB_DIM), vec(SUB_DIM), vec(SUB_DIM), vec(SUB_DIM), vec(HEAD_DIM),
            ],
            out_specs=pl.BlockSpec((QB, width), lambda b, i, l: (b * nq + i, 0)),
            scratch_shapes=[pltpu.VMEM((QB, far_max), F32), pltpu.VMEM((QB, WIN), F32)]),
        out_shape=jax.ShapeDtypeStruct((m_total, width), BF16),
        compiler_params=_cparams("parallel", "arbitrary"),
        name="diff_prompt",
    )(lidx, cfar, lam_inits, projq, pkv_b, pkv_b, dwin, lq1, lk1, lq2, lk2, subg)


def _dsa_sample_body(topk, l_ref, qa_ref, qi_ref, wi_ref, kn_ref, vn_ref, kxn_ref, kc_ref, vc_ref, kxc_ref,
                     bc_ref, bn_ref, oin_ref, o_ref, mc_ref, mn_ref):
    s_c, s_n = _index_scores(qi_ref[...], wi_ref[...], [kxc_ref[...].astype(BF16), kxn_ref[...]])
    key_c = _sortable(s_c)
    key_n = _sortable(s_n)
    thr = _kth_largest([key_c, key_n], topk)
    mc_ref[...] = jnp.where(key_c >= thr, 0.0, NEG)
    mn_ref[...] = jnp.where(key_n >= thr, 0.0, NEG)
    k_c = kc_ref[...].astype(BF16)
    v_c = vc_ref[...].astype(BF16)
    k_n = kn_ref[...]
    v_n = vn_ref[...]
    c1 = HEAD_DIM ** -0.5 * LOG2E

    def head(h, carry):
        off = pl.multiple_of(h * HEAD_DIM, HEAD_DIM)
        q = qa_ref[:, pl.ds(off, HEAD_DIM)]
        tc = _dot_nt(q, k_c) * c1 + (bc_ref[h] + mc_ref[...])
        tn = _dot_nt(q, k_n) * c1 + (bn_ref[h] + mn_ref[...])
        (ec, en), rl = _softmax2([tc, tn], [None, None])
        o = (_dot(ec.astype(BF16), v_c) + _dot(en.astype(BF16), v_n)) * rl
        o_ref[:, pl.ds(off, HEAD_DIM)] = o.astype(o_ref.dtype)
        return carry

    lax.fori_loop(0, N_HEADS, head, 0)


def _dsa_sample(lidx, projq, pkv_b, pkv_f, cache_ka, cache_va, cache_kx2, bias_c, bias_n, o_all, row0, cols):
    _, nb, past, _ = cache_ka.shape
    t = bias_c.shape[1]
    r0 = row0 // t
    topk = min(TOPK_MAX, (past + t) // 4)
    width = N_HEADS * HEAD_DIM
    w128 = lambda key: (lambda b, l: (r0 + b, cols[key]))
    cache = lambda n: pl.BlockSpec((None, None, past, n), lambda b, l: (l[0], b, 0, 0))
    return pl.pallas_call(
        functools.partial(_dsa_sample_body, topk),
        grid_spec=pltpu.PrefetchScalarGridSpec(
            num_scalar_prefetch=1, grid=(nb,),
            in_specs=[
                pl.BlockSpec((t, width), w128("qa")),
                pl.BlockSpec((t, N_IDX_HEADS * IDX_DIM), w128("qi")),
                pl.BlockSpec((t, LANES), w128("wi")),
                pl.BlockSpec((t, HEAD_DIM), w128("ka")),
                pl.BlockSpec((t, HEAD_DIM), w128("va")),
                pl.BlockSpec((t, LANES), w128("kx")),
                cache(HEAD_DIM), cache(HEAD_DIM), cache(LANES),
                pl.BlockSpec((N_HEADS, t, past), lambda b, l: (0, 0, 0)),
                pl.BlockSpec((N_HEADS, t, t), lambda b, l: (0, 0, 0)),
                pl.BlockSpec(memory_space=pl.ANY),
            ],
            out_specs=pl.BlockSpec((t, width), lambda b, l: (r0 + b, 0)),
            scratch_shapes=[pltpu.VMEM((t, past), F32), pltpu.VMEM((t, t), F32)]),
        out_shape=jax.ShapeDtypeStruct(o_all.shape, BF16),
        input_output_aliases={12: 0},
        compiler_params=_cparams("arbitrary"),
        name="dsa_sample",
    )(lidx, projq, projq, pkv_f, pkv_b, pkv_b, pkv_b, cache_ka, cache_va, cache_kx2, bias_c, bias_n, o_all)


def _diff_sample_body(l_ref, laminit_ref, qb_ref, kn_ref, vn_ref, kc_ref, vc_ref, bc_ref, bn_ref,
                      lq1_ref, lk1_ref, lq2_ref, lk2_ref, g_ref, oin_ref, o_ref):
    lam_init = laminit_ref[l_ref[0]]
    lam = _lambda(lq1_ref[...], lk1_ref[...], lq2_ref[...], lk2_ref[...], lam_init)
    c1 = SUB_DIM ** -0.5 * LOG2E
    g = g_ref[...]

    def head(h, carry):
        off = pl.multiple_of(h * HEAD_DIM, HEAD_DIM)
        hc = pl.ds(off, HEAD_DIM)
        q1, q2 = _lane_half_masks(qb_ref[:, hc])
        k_c = kc_ref[:, hc].astype(BF16)
        v_c = vc_ref[:, hc].astype(BF16)
        k_n = kn_ref[:, hc]
        v_n = vn_ref[:, hc]
        bc = bc_ref[h]
        bn = bn_ref[h]
        none2 = [None, None]
        (e1c, e1n), r1 = _softmax2([_dot_nt(q1, k_c) * c1 + bc, _dot_nt(q1, k_n) * c1 + bn], none2)
        (e2c, e2n), r2 = _softmax2([_dot_nt(q2, k_c) * c1 + bc, _dot_nt(q2, k_n) * c1 + bn], none2)
        r2 = r2 * lam
        o = (_dot((e1c * r1 - e2c * r2).astype(BF16), v_c) + _dot((e1n * r1 - e2n * r2).astype(BF16), v_n))
        o_ref[:, hc] = _sub_ln(o, g, lam_init).astype(o_ref.dtype)
        return carry

    lax.fori_loop(0, N_HEADS, head, 0)


def _diff_sample(lidx, projq, pkv_b, cache_kb, cache_vb, bias_c, bias_n, lam_inits, lq1, lk1, lq2, lk2, subg,
                 o_all, row0, cols):
    _, nb, past, width = cache_kb.shape
    t = bias_c.shape[1]
    r0 = row0 // t
    vec = lambda n: pl.BlockSpec((None, 1, n), lambda b, l: (l[0], 0, 0))
    cache = pl.BlockSpec((None, None, past, width), lambda b, l: (l[0], b, 0, 0))
    return pl.pallas_call(
        _diff_sample_body,
        grid_spec=pltpu.PrefetchScalarGridSpec(
            num_scalar_prefetch=1, grid=(nb,),
            in_specs=[
                pl.BlockSpec(memory_space=pltpu.SMEM),
                pl.BlockSpec((t, width), lambda b, l: (r0 + b, cols["qb"])),
                pl.BlockSpec((t, width), lambda b, l: (r0 + b, cols["kb"])),
                pl.BlockSpec((t, width), lambda b, l: (r0 + b, cols["vb"])),
                cache, cache,
                pl.BlockSpec((N_HEADS, t, past), lambda b, l: (0, 0, 0)),
                pl.BlockSpec((N_HEADS, t, t), lambda b, l: (0, 0, 0)),
                vec(SUB_DIM), vec(SUB_DIM), vec(SUB_DIM), vec(SUB_DIM), vec(HEAD_DIM),
                pl.BlockSpec(memory_space=pl.ANY),
            ],
            out_specs=pl.BlockSpec((t, width), lambda b, l: (r0 + b, 0))),
        out_shape=jax.ShapeDtypeStruct(o_all.shape, BF16),
        input_output_aliases={14: 0},
        compiler_params=_cparams("arbitrary"),
        name="diff_sample",
    )(lidx, lam_inits, projq, pkv_b, pkv_b, cache_kb, cache_vb, bias_c, bias_n, lq1, lk1, lq2, lk2, subg, o_all)


def _layer_norm_rows(z, g, b):
    mu = jnp.mean(z, axis=-1, keepdims=True)
    zc = z - mu
    var = jnp.mean(zc * zc, axis=-1, keepdims=True)
    return zc * lax.rsqrt(var + LN_EPS) * g + b


def _route(rl):
    lane = lax.broadcasted_iota(jnp.int32, rl.shape, 1).astype(F32)
    big = 1e6
    gl = jnp.where(lane < N_GROUPS, rl, NEG)
    gmax = gl.max(axis=-1, keepdims=True)
    gsum = jnp.sum(jnp.exp(gl - gmax), axis=-1, keepdims=True)
    g_val = 1.0 / gsum
    g_idx = jnp.min(jnp.where(gl == gmax, lane, big), axis=-1, keepdims=True)
    lo = N_GROUPS + g_idx * EXPERTS_PER_GROUP
    el = jnp.where((lane >= lo) & (lane < lo + EXPERTS_PER_GROUP), rl, NEG)
    m1 = el.max(axis=-1, keepdims=True)
    i1 = jnp.min(jnp.where(el == m1, lane, big), axis=-1, keepdims=True)
    el2 = jnp.where(lane == i1, NEG, el)
    m2 = el2.max(axis=-1, keepdims=True)
    i2 = jnp.min(jnp.where(el2 == m2, lane, big), axis=-1, keepdims=True)
    e2 = jnp.exp(m2 - m1)
    den = 1.0 / (1.0 + e2)
    w1 = den * g_val
    w2 = e2 * den * g_val
    out = jnp.where(lane == 0, i1 - N_GROUPS, 0.0)
    out = jnp.where(lane == 1, i2 - N_GROUPS, out)
    out = jnp.where(lane == 2, w1, out)
    out = jnp.where(lane == 3, w2, out)
    return out


HI16 = -65536


def _pack_bf16_pair(x):
    n = x.shape[1] // 2
    bits = pltpu.bitcast(x.astype(BF16).astype(F32), jnp.int32)
    return lax.shift_right_logical(bits[:, :n], 16) | (bits[:, n:] & HI16)


def _unpack_bf16_pair(p):
    return pltpu.bitcast(lax.shift_left(p, 16), F32), pltpu.bitcast(p & HI16, F32)


def _out_ln_body(alpha, tn, rows, l_ref, oa_ref, ob_ref, wa_ref, wb_ref, x_ref, g_ref, b_ref, wr_ref,
                 h_ref, hp_ref, rt_ref):
    j = pl.program_id(1)
    acc = _dot(oa_ref[...], wa_ref[...]) + _dot(ob_ref[...], wb_ref[...])
    off = pl.multiple_of(j * tn, tn)
    h_ref[:, pl.ds(off, tn)] = alpha * x_ref[...] + acc

    @pl.when(j == pl.num_programs(1) - 1)
    def _():
        g = g_ref[...]
        b = b_ref[...]

        def chunk(c, carry):
            r = pl.multiple_of(c * rows, rows)
            hn = _layer_norm_rows(h_ref[pl.ds(r, rows), :], g, b)
            h_ref[pl.ds(r, rows), :] = hn
            hp_ref[pl.ds(r, rows), :] = _pack_bf16_pair(hn)
            rt_ref[pl.ds(r, rows), :] = _dot(hn.astype(BF16), wr_ref[...])
            return carry

        lax.fori_loop(0, h_ref.shape[0] // rows, chunk, 0)
        rt_ref[...] = _route(rt_ref[...])


def _out_ln(lidx, oa, ob, w_out, x, g, b, wr, alpha, tm, tn):
    m, half = oa.shape
    d = w_out.shape[-1]
    rows = next(r for r in (64, 48, 32, 16, 8) if tm % r == 0)
    return pl.pallas_call(
        functools.partial(_out_ln_body, alpha, tn, rows),
        grid_spec=pltpu.PrefetchScalarGridSpec(
            num_scalar_prefetch=1, grid=(m // tm, d // tn),
            in_specs=[
                pl.BlockSpec((tm, half), lambda i, j, l: (i, 0)),
                pl.BlockSpec((tm, half), lambda i, j, l: (i, 0)),
                pl.BlockSpec((None, half, tn), lambda i, j, l: (l[0], 0, j)),
                pl.BlockSpec((None, half, tn), lambda i, j, l: (l[0], 1, j)),
                pl.BlockSpec((tm, tn), lambda i, j, l: (i, j)),
                pl.BlockSpec((None, 1, d), lambda i, j, l: (l[0], 0, 0)),
                pl.BlockSpec((None, 1, d), lambda i, j, l: (l[0], 0, 0)),
                pl.BlockSpec((None, d, LANES), lambda i, j, l: (l[0], 0, 0)),
            ],
            out_specs=[pl.BlockSpec((tm, d), lambda i, j, l: (i, 0)),
                       pl.BlockSpec((tm, d // 2), lambda i, j, l: (i, 0)),
                       pl.BlockSpec((tm, LANES), lambda i, j, l: (i, 0))]),
        out_shape=[jax.ShapeDtypeStruct((m, d), F32), jax.ShapeDtypeStruct((m, d // 2), jnp.int32),
                   jax.ShapeDtypeStruct((m, LANES), F32)],
        compiler_params=_cparams("parallel", "arbitrary"),
        name="out_proj_ln_route",
    )(lidx, oa, ob, w_out, w_out, x, g, b, wr)


def _moe_body(tm, spare_row, meta_ref, te_ref, src_ref, nx1_ref, nx2_ref, dst_ref, prv_ref, gate_ref, h_hbm,
              w1_ref, w3_ref, w2_ref, y_hbm, xbuf, obuf, xb, gsem, ssem):
    t = pl.program_id(0)
    n_used = meta_ref[1]
    slot = lax.rem(t, 2)
    other = 1 - slot
    g0 = lax.rem(t, 3)
    g1 = lax.rem(t + 1, 3)
    g2 = lax.rem(t + 2, 3)

    def wait_gather(s):
        pltpu.make_async_copy(h_hbm.at[pl.ds(0, tm)], xbuf.at[s], gsem.at[s]).wait()

    def wait_scatter(s):
        pltpu.make_async_copy(obuf.at[s], y_hbm.at[pl.ds(0, tm)], ssem.at[s]).wait()

    @pl.when(t == 0)
    def _():
        def row(r, carry):
            pltpu.make_async_copy(h_hbm.at[pl.ds(src_ref[0, r], 1)], xbuf.at[0, pl.ds(r, 1)], gsem.at[0]).start()
            pltpu.make_async_copy(h_hbm.at[pl.ds(nx1_ref[0, r], 1)], xbuf.at[1, pl.ds(r, 1)], gsem.at[1]).start()
            return carry
        lax.fori_loop(0, tm, row, 0, unroll=8)
        obuf[...] = jnp.zeros_like(obuf)
        pltpu.make_async_copy(obuf.at[0], y_hbm.at[pl.ds(spare_row, tm)], ssem.at[0]).start()

    @pl.when(t < n_used)
    def _():
        wait_gather(g0)
        wait_scatter(slot)
        lo, hi = _unpack_bf16_pair(xbuf[g0])
        half = lo.shape[1]
        xb[:, 0:half] = lo.astype(BF16)
        xb[:, half:2 * half] = hi.astype(BF16)
        for r in range(tm):
            pltpu.make_async_copy(h_hbm.at[pl.ds(nx2_ref[0, r], 1)], xbuf.at[g2, pl.ds(r, 1)],
                                  gsem.at[g2]).start()
        for r in range(tm):
            pltpu.make_async_copy(obuf.at[other, pl.ds(r, 1)], y_hbm.at[pl.ds(prv_ref[0, r], 1)],
                                  ssem.at[other]).start()
        x = xb[...]
        u = _dot(x, w1_ref[...])
        v = _dot(x, w3_ref[...])
        a = (u * (1.0 / (1.0 + jnp.exp(-u))) * v).astype(BF16)
        obuf[slot] = _pack_bf16_pair(_dot(a, w2_ref[...]) * gate_ref[:, 0:1])

        @pl.when(t + 1 >= n_used)
        def _():
            def row(r, carry):
                pltpu.make_async_copy(obuf.at[slot, pl.ds(r, 1)], y_hbm.at[pl.ds(dst_ref[0, r], 1)],
                                      ssem.at[slot]).start()
                return carry
            lax.fori_loop(0, tm, row, 0, unroll=8)
            wait_gather(g1)
            wait_gather(g2)
            wait_scatter(other)
            wait_scatter(slot)


def _moe(meta, tile_expert, src3, dst3, prv3, gate, hp, w1, w3, w2, tm, y_rows):
    d, f = w1.shape[-2:]
    dp = hp.shape[1]
    nt = src3.shape[0]
    spare_row = y_rows - tm
    idx_spec = lambda imap: pl.BlockSpec((None, 1, tm), imap, memory_space=pltpu.SMEM)
    wspec = lambda shape, nbuf: pl.BlockSpec((None, None) + shape, lambda t, meta, te: (meta[0], te[t], 0, 0),
                                             pipeline_mode=pl.Buffered(nbuf))
    return pl.pallas_call(
        functools.partial(_moe_body, tm, spare_row),
        grid_spec=pltpu.PrefetchScalarGridSpec(
            num_scalar_prefetch=2, grid=(nt,),
            in_specs=[
                idx_spec(lambda t, meta, te: (t, 0, 0)),
                idx_spec(lambda t, meta, te: (jnp.minimum(t + 1, nt - 1), 0, 0)),
                idx_spec(lambda t, meta, te: (jnp.minimum(t + 2, nt - 1), 0, 0)),
                idx_spec(lambda t, meta, te: (t, 0, 0)),
                idx_spec(lambda t, meta, te: (t, 0, 0)),
                pl.BlockSpec((tm, LANES), lambda t, meta, te: (t, 0)),
                pl.BlockSpec(memory_space=pl.ANY),
                wspec((d, f), 1), wspec((d, f), 1), wspec((f, d), 2),
            ],
            out_specs=pl.BlockSpec(memory_space=pl.ANY),
            scratch_shapes=[pltpu.VMEM((3, tm, dp), jnp.int32), pltpu.VMEM((2, tm, dp), jnp.int32),
                            pltpu.VMEM((tm, d), BF16),
                            pltpu.SemaphoreType.DMA((3,)), pltpu.SemaphoreType.DMA((2,))]),
        out_shape=jax.ShapeDtypeStruct((y_rows, dp), jnp.int32),
        compiler_params=_cparams("arbitrary"),
        name="moe_experts",
    )(meta, tile_expert, src3, src3, src3, dst3, prv3, gate, hp, w1, w3, w2)


def _dispatch_plan(route, tm):
    m = route.shape[0]
    npairs = 2 * m
    nt = npairs // tm + N_EXPERTS
    p = nt * tm
    e = route[:, 0:2].astype(jnp.int32).reshape(npairs)
    wgt = route[:, 2:4].reshape(npairs)
    order = jnp.argsort(e, stable=True).astype(jnp.int32)
    counts = jnp.sum((e[:, None] == jnp.arange(N_EXPERTS, dtype=jnp.int32)[None, :]).astype(jnp.int32), axis=0)
    padded = ((counts + tm - 1) // tm) * tm
    pad_end = jnp.cumsum(padded)
    pad_start = pad_end - padded
    cnt_start = jnp.cumsum(counts) - counts
    tile_start = jnp.arange(nt, dtype=jnp.int32) * tm
    tile_expert = jnp.minimum(jnp.sum((pad_end[None, :] <= tile_start[:, None]).astype(jnp.int32), axis=1),
                              N_EXPERTS - 1)
    slot = jnp.arange(p, dtype=jnp.int32)
    se = jnp.repeat(tile_expert, tm)
    rank = slot - pad_start[se]
    valid = (rank >= 0) & (rank < counts[se])
    pair = order[jnp.clip(cnt_start[se] + rank, 0, npairs - 1)]
    src = jnp.where(valid, pair // 2, 0)
    dst = jnp.where(valid, (pair % 2) * m + pair // 2, npairs + slot)
    gate = jnp.where(valid, wgt[pair], 0.0)
    n_used = (pad_end[-1] // tm).astype(jnp.int32)
    gate128 = jnp.broadcast_to(gate[:, None], (p, LANES))
    dst3 = dst.reshape(nt, 1, tm)
    spare = (npairs + p + jnp.arange(tm, dtype=jnp.int32)).reshape(1, 1, tm)
    prv3 = jnp.concatenate([spare, dst3[:-1]], axis=0)
    return src.reshape(nt, 1, tm), dst3, prv3, gate128, tile_expert.astype(jnp.int32), n_used


def _ln2_body(alpha, l_ref, h_ref, y0_ref, y1_ref, g_ref, b_ref, o_ref, ob_ref):
    lo0, hi0 = _unpack_bf16_pair(y0_ref[...])
    lo1, hi1 = _unpack_bf16_pair(y1_ref[...])
    z = alpha * h_ref[...] + jnp.concatenate([lo0 + lo1, hi0 + hi1], axis=1)
    hn = _layer_norm_rows(z, g_ref[...], b_ref[...])
    o_ref[...] = hn
    ob_ref[...] = hn.astype(BF16)


def _ln2(lidx, h, y, g, b, alpha, tm):
    m, d = h.shape
    nblk = m // tm
    return pl.pallas_call(
        functools.partial(_ln2_body, alpha),
        grid_spec=pltpu.PrefetchScalarGridSpec(
            num_scalar_prefetch=1, grid=(nblk,),
            in_specs=[
                pl.BlockSpec((tm, d), lambda i, l: (i, 0)),
                pl.BlockSpec((tm, d // 2), lambda i, l: (i, 0)),
                pl.BlockSpec((tm, d // 2), lambda i, l: (i + nblk, 0)),
                pl.BlockSpec((None, 1, d), lambda i, l: (l[0], 0, 0)),
                pl.BlockSpec((None, 1, d), lambda i, l: (l[0], 0, 0)),
            ],
            out_specs=[pl.BlockSpec((tm, d), lambda i, l: (i, 0)),
                       pl.BlockSpec((tm, d), lambda i, l: (i, 0))]),
        out_shape=[jax.ShapeDtypeStruct((m, d), F32), jax.ShapeDtypeStruct((m, d), BF16)],
        compiler_params=_cparams("parallel"),
        name="moe_combine_ln",
    )(lidx, h, y, y, g, b)


def _pick_tile(m, candidates):
    for c in candidates:
        if m % c == 0:
            return c
    raise ValueError(f"no row tile for {m} rows")


def kernel(x_prompt, x_sample, cache_ka, cache_va, cache_kidx, cache_kb, cache_vb, rel_bias, w_in, w_out,
           lambda_q1, lambda_k1, lambda_q2, lambda_k2, subln_g, ln1_g, ln1_b, ln2_g, ln2_b,
           w_route_group, w_route_expert, w1, w3, w2):
    batch, seq, d = x_prompt.shape
    dec_batch, dec_seq, _ = x_sample.shape
    depth = w_in.shape[0]
    past = cache_ka.shape[2]
    mp = batch * seq
    ms = dec_batch * dec_seq
    m = mp + ms
    alpha = (2.0 * depth) ** 0.25
    width = N_HEADS * HEAD_DIM
    assert seq % QB == 0 and seq >= 2 * WIN and mp % dec_seq == 0 and width == 2048

    in_widths = (width, HEAD_DIM, HEAD_DIM, N_IDX_HEADS * IDX_DIM, IDX_DIM, N_IDX_HEADS, width, width, width)
    splits = [int(s) for s in np.cumsum(in_widths)[:-1]]
    wqa, wka, wva, wqi, wki, wwi, wqb, wkb, wvb = jnp.split(w_in.astype(BF16), splits, axis=-1)
    w_q = jnp.concatenate([wqa, wqb, wqi], axis=-1)
    pad = jnp.zeros(wwi.shape[:-1] + (LANES - N_IDX_HEADS,), BF16)
    w_kv = jnp.concatenate([wkb, wvb, wka, wva, wki, wki, wwi, pad], axis=-1)
    n_kv = w_kv.shape[-1]
    cols_q = {"qa": 0, "qb": 1, "qi": 4}
    cols_kv = {"kb": 0, "vb": 1, "ka": 32, "va": 33, "kx": 34, "wi": 35}
    cols = {**cols_q, **cols_kv}
    w_out_b = w_out.astype(BF16)
    w1_b = w1.astype(BF16)
    w3_b = w3.astype(BF16)
    w2_b = w2.astype(BF16)
    w_r = jnp.concatenate([w_route_group, w_route_expert,
                           jnp.zeros((depth, d, LANES - N_GROUPS - N_EXPERTS), F32)], axis=-1).astype(BF16)
    vec3 = lambda a: a.reshape(depth, 1, a.shape[-1])
    lq1, lk1, lq2, lk2, subg = map(vec3, (lambda_q1, lambda_k1, lambda_q2, lambda_k2, subln_g))
    g1, b1, g2, b2 = map(vec3, (ln1_g, ln1_b, ln2_g, ln2_b))
    lam_inits = jnp.asarray([0.8 - 0.6 * math.exp(-0.3 * l) for l in range(depth)], F32)

    dwin, cfar = _prompt_bias_tiles(rel_bias)
    sbias = _sample_bias(rel_bias, past, dec_seq)
    dwin_a, dwin_b = dwin[:N_HEADS], dwin[N_HEADS:]
    cfar_a, cfar_b = cfar[:N_HEADS], cfar[N_HEADS:]
    sb_a_c, sb_a_n = sbias[:N_HEADS, :, :past], sbias[:N_HEADS, :, past:]
    sb_b_c, sb_b_n = sbias[N_HEADS:, :, :past], sbias[N_HEADS:, :, past:]
    cache_kb2 = cache_kb.astype(BF16).reshape(cache_kb.shape[:3] + (width,))
    cache_vb2 = cache_vb.astype(BF16).reshape(cache_vb.shape[:3] + (width,))
    cache_kx2 = jnp.concatenate([cache_kidx, cache_kidx], axis=-1)

    tm_proj = _pick_tile(m, (768, 512, 256, 128, 64, 32))
    tm_out = _pick_tile(m, (528, 384, 256, 128, 64, 32))
    tm_ln2 = _pick_tile(m, (256, 128, 64, 32))
    tm_moe = 256 if m >= 1024 else 32
    y_rows = 2 * m + (2 * m // tm_moe + N_EXPERTS) * tm_moe + 2 * tm_moe

    x0 = jnp.concatenate([x_prompt.reshape(mp, d), x_sample.reshape(ms, d)], axis=0)
    x, xb = x0, x0.astype(BF16)
    rows = []
    for layer in range(depth):
        lidx = jnp.full((1,), layer, jnp.int32)
        projq = _proj(lidx, xb, w_q, tm_proj, 512, False)[0]
        pkv_b, pkv_f = _proj(lidx, xb, w_kv, tm_proj, 512, True)
        oa = _dsa_prompt(lidx, projq, pkv_b, pkv_f, dwin_a, cfar_a, batch, seq, m, cols)
        ob = _diff_prompt(lidx, projq, pkv_b, dwin_b, cfar_b, lam_inits, lq1, lk1, lq2, lk2, subg,
                          batch, seq, m, cols)
        oa = _dsa_sample(lidx, projq, pkv_b, pkv_f, cache_ka, cache_va, cache_kx2, sb_a_c, sb_a_n, oa, mp, cols)
        ob = _diff_sample(lidx, projq, pkv_b, cache_kb2, cache_vb2, sb_b_c, sb_b_n, lam_inits,
                          lq1, lk1, lq2, lk2, subg, ob, mp, cols)
        h1, h1p, route = _out_ln(lidx, oa, ob, w_out_b, x, g1, b1, w_r, alpha, tm_out, 512)
        src3, dst3, prv3, gate128, tile_expert, n_used = _dispatch_plan(route, tm_moe)
        meta = jnp.stack([lidx[0], n_used])
        y = _moe(meta, tile_expert, src3, dst3, prv3, gate128, h1p, w1_b, w3_b, w2_b, tm_moe, y_rows)
        x, xb = _ln2(lidx, h1, y, g2, b2, alpha, tm_ln2)
        rows.append(pkv_f)
    y = x
    heads = (N_HEADS, 2 * SUB_DIM)

    def split(c0, n, tail):
        return (jnp.stack([r[:mp, c0:c0 + n] for r in rows]).reshape((depth, batch, seq) + tail),
                jnp.stack([r[mp:, c0:c0 + n] for r in rows]).reshape((depth, dec_batch, dec_seq) + tail))

    p_kb, s_kb = split(0, width, heads)
    p_vb, s_vb = split(width, width, heads)
    p_ka, s_ka = split(2 * width, HEAD_DIM, (HEAD_DIM,))
    p_va, s_va = split(2 * width + HEAD_DIM, HEAD_DIM, (HEAD_DIM,))
    p_kx, s_kx = split(2 * width + 2 * HEAD_DIM, IDX_DIM, (IDX_DIM,))
    y_prompt = y[:mp].reshape(batch, seq, d)
    y_sample = y[mp:].reshape(dec_batch, dec_seq, d)
    return (y_prompt, y_sample, p_ka, p_va, p_kx, p_kb, p_vb, s_ka, s_va, s_kx, s_kb, s_vb)
```

```python
import functools
import math

import numpy as np
import jax
import jax.numpy as jnp
from jax import lax
from jax.experimental import pallas as pl
from jax.experimental.pallas import tpu as pltpu

CHUNK = 64
HEAD_DIM = 128
N_HEADS = 16
N_IDX_HEADS = 16
IDX_DIM = 64
SUB_DIM = 64
TOPK_MAX = 256
NUM_BUCKETS = 32
MAX_DISTANCE = 128
N_GROUPS = 4
EXPERTS_PER_GROUP = 4
N_EXPERTS = 16
LN_EPS = 1e-5
RMS_EPS = 1e-5

LANES = 128
QB = 128
WIN = 2 * QB
FAR_STEP = 512
HEAD_UNROLL = 4
NEG = -1e30
INT_MIN = -2147483648
LOG2E = 1.4426950408889634
VMEM_LIMIT = 56 * 1024 * 1024

F32 = jnp.float32
BF16 = jnp.bfloat16


def _cparams(*sem):
    return pltpu.CompilerParams(dimension_semantics=sem, vmem_limit_bytes=VMEM_LIMIT)


def _dot_nt(a, b):
    return lax.dot_general(a, b, (((1,), (1,)), ((), ())), preferred_element_type=F32)


def _dot(a, b):
    return jnp.dot(a, b, preferred_element_type=F32)


def _bucket_np(rel):
    nb = NUM_BUCKETS // 2
    ret = np.where(rel > 0, nb, 0)
    n = np.abs(rel)
    max_exact = nb // 2
    nf = np.maximum(n, 1).astype(np.float32)
    frac = np.log(nf / np.float32(max_exact)) / np.float32(math.log(MAX_DISTANCE / max_exact))
    large = max_exact + (frac.astype(np.float32) * np.float32(nb - max_exact)).astype(np.int32)
    large = np.minimum(large, nb - 1)
    return (ret + np.where(n < max_exact, n, large)).astype(np.int32)


def _bias_lookup(rel_bias, bucket):
    onehot = (jnp.asarray(bucket.reshape(-1, 1)) == jnp.arange(NUM_BUCKETS, dtype=jnp.int32)[None, :]).astype(F32)
    vals = jnp.dot(onehot, rel_bias.astype(F32), precision=lax.Precision.HIGHEST)
    return jnp.transpose(vals).reshape((rel_bias.shape[1],) + bucket.shape)


def _prompt_bias_tiles(rel_bias):
    ql = np.arange(QB)[:, None]
    c = np.arange(WIN)[None, :]
    rel = c - QB - ql
    kl = c - QB
    adm = (c < QB) | ((kl // CHUNK) <= (ql // CHUNK))
    tiles = _bias_lookup(rel_bias, _bucket_np(rel)) * LOG2E
    tiles = jnp.where(jnp.asarray(adm)[None], tiles, NEG)
    far = rel_bias[int(_bucket_np(np.array([-(QB + 1)]))[0])] * LOG2E
    return tiles.astype(F32), far.astype(F32)


def _sample_bias(rel_bias, past_len, dec_seq):
    q = past_len + np.arange(dec_seq)[:, None]
    k = np.arange(past_len + dec_seq)[None, :]
    return (_bias_lookup(rel_bias, _bucket_np(k - q)) * LOG2E).astype(F32)


def _proj_body(l_ref, x_ref, w_ref, ob_ref, *of_ref):
    acc = _dot(x_ref[...], w_ref[...])
    ob_ref[...] = acc.astype(BF16)
    if of_ref:
        of_ref[0][...] = acc


def _proj(lidx, xb, w, tm, tn, with_f32):
    m, d = xb.shape
    n = w.shape[-1]
    out_shape = [jax.ShapeDtypeStruct((m, n), BF16)]
    out_specs = [pl.BlockSpec((tm, tn), lambda i, j, l: (i, j))]
    if with_f32:
        out_shape.append(jax.ShapeDtypeStruct((m, n), F32))
        out_specs.append(pl.BlockSpec((tm, tn), lambda i, j, l: (i, j)))
    return pl.pallas_call(
        _proj_body,
        grid_spec=pltpu.PrefetchScalarGridSpec(
            num_scalar_prefetch=1, grid=(m // tm, n // tn),
            in_specs=[pl.BlockSpec((tm, d), lambda i, j, l: (i, 0)),
                      pl.BlockSpec((None, d, tn), lambda i, j, l: (l[0], 0, j))],
            out_specs=out_specs),
        out_shape=out_shape,
        compiler_params=_cparams("parallel", "arbitrary"),
        name="in_proj_rows" if with_f32 else "in_proj",
    )(lidx, xb, w)


def _sortable(score):
    bits = pltpu.bitcast(score + 0.0, jnp.int32)
    return jnp.where(bits < 0, bits ^ jnp.int32(0x7FFFFFFF), bits)


def _kth_largest(keys, k):
    t = keys[0].shape[0]

    def count_ge(cand):
        cnt = jnp.zeros((t, 1), F32)
        for kk in keys:
            cnt = cnt + jnp.sum(jnp.where(kk >= cand, 1.0, 0.0), axis=-1, keepdims=True)
        return cnt

    def step(it, thr):
        hi = jnp.left_shift(jnp.int32(1), 31 - 2 * it)
        lo = jnp.left_shift(jnp.int32(1), 30 - 2 * it)
        c1 = thr + lo
        c2 = thr + hi
        c3 = c2 + lo
        n1, n2, n3 = count_ge(c1), count_ge(c2), count_ge(c3)
        return jnp.where(n3 >= k, c3, jnp.where(n2 >= k, c2, jnp.where(n1 >= k, c1, thr)))

    return lax.fori_loop(0, 16, step, jnp.full((t, 1), INT_MIN, jnp.int32))


def _index_scores(qi, wi, kparts):
    t = qi.shape[0]
    lane = lax.broadcasted_iota(jnp.int32, (t, LANES), 1)
    w = wi * (N_IDX_HEADS ** -0.5 * IDX_DIM ** -0.5)
    acc = [jnp.zeros((t, kp.shape[0]), F32) for kp in kparts]
    for p in range(N_IDX_HEADS // 2):
        qp = qi[:, p * LANES:(p + 1) * LANES]
        halves = (jnp.where(lane < IDX_DIM, qp, jnp.zeros_like(qp)),
                  jnp.where(lane >= IDX_DIM, qp, jnp.zeros_like(qp)))
        for s, qh in enumerate(halves):
            h = 2 * p + s
            wh = w[:, h:h + 1]
            for j, kp in enumerate(kparts):
                acc[j] = acc[j] + jnp.maximum(_dot_nt(qh, kp), 0.0) * wh
    return acc


def _softmax2(parts, shifts):
    m = None
    for tp, c in zip(parts, shifts):
        mj = tp.max(axis=-1, keepdims=True)
        if c is not None:
            mj = mj + c
        m = mj if m is None else jnp.maximum(m, mj)
    es, l = [], None
    for tp, c in zip(parts, shifts):
        e = jnp.exp2(tp - (m if c is None else m - c))
        es.append(e)
        lj = e.sum(axis=-1, keepdims=True)
        l = lj if l is None else l + lj
    return es, 1.0 / l


def _lane_half_masks(q):
    lane = lax.broadcasted_iota(jnp.int32, q.shape, 1)
    z = jnp.zeros_like(q)
    return jnp.where(lane < SUB_DIM, q, z), jnp.where(lane >= SUB_DIM, q, z)


def _lambda(lq1, lk1, lq2, lk2, lam_init):
    return (jnp.exp(jnp.sum(lq1 * lk1, axis=-1, keepdims=True))
            - jnp.exp(jnp.sum(lq2 * lk2, axis=-1, keepdims=True)) + lam_init)


def _sub_ln(o, g, lam_init):
    return o * lax.rsqrt(jnp.mean(o * o, axis=-1, keepdims=True) + RMS_EPS) * g * (1.0 - lam_init)


def _far_buckets(nq, far_max):
    lens = list(range(0, far_max, FAR_STEP)) + [far_max]
    if len(lens) > 2 and lens[-1] - lens[-2] < FAR_STEP:
        del lens[-2]
    out, lo = [], 0
    for fl in lens:
        hi = min(nq, fl // QB + 2)
        if hi > lo:
            out.append((lo, hi, fl))
            lo = hi
    assert lo == nq
    return out


def _window(ref, prev0, diag0, cols=slice(None)):
    return jnp.concatenate([ref[pl.ds(prev0, QB), cols], ref[pl.ds(diag0, QB), cols]], axis=0)


def _dsa_block(fl, topk, i, cfar_ref, qa_ref, qi_ref, wi_ref, k_ref, v_ref, kx_ref, dwin_ref,
               o_ref, mfar_ref, mwin_ref):
    prev0 = pl.multiple_of(jnp.maximum(i - 1, 0) * QB, QB)
    diag0 = pl.multiple_of(i * QB, QB)

    kparts = ([kx_ref[0:fl, :]] if fl else []) + [_window(kx_ref, prev0, diag0)]
    scores = _index_scores(qi_ref[...], wi_ref[...], kparts)
    s_win = scores[-1]
    col_win = lax.broadcasted_iota(jnp.int32, s_win.shape, 1)
    row_win = lax.broadcasted_iota(jnp.int32, s_win.shape, 0)
    ok_win = ((col_win < QB + CHUNK) | (row_win >= CHUNK)) & ((col_win >= QB) | (i > 0))
    key_win = jnp.where(ok_win, _sortable(s_win), INT_MIN)
    keys = [key_win]
    if fl:
        s_far = scores[0]
        col_far = lax.broadcasted_iota(jnp.int32, s_far.shape, 1)
        ok_far = col_far < (i - 1) * QB
        key_far = jnp.where(ok_far, _sortable(s_far), INT_MIN)
        keys = [key_far, key_win]
    thr = _kth_largest(keys, topk)
    mwin_ref[...] = jnp.where(ok_win & (key_win >= thr), 0.0, NEG)
    if fl:
        mfar_ref[:, 0:fl] = jnp.where(ok_far & (key_far >= thr), 0.0, NEG)

    k_win = _window(k_ref, prev0, diag0)
    v_win = _window(v_ref, prev0, diag0)
    c1 = HEAD_DIM ** -0.5 * LOG2E

    def head(h, carry):
        off = pl.multiple_of(h * HEAD_DIM, HEAD_DIM)
        q = qa_ref[:, pl.ds(off, HEAD_DIM)]
        tw = _dot_nt(q, k_win) * c1 + (dwin_ref[h] + mwin_ref[...])
        if fl:
            tf = _dot_nt(q, k_ref[0:fl, :]) * c1 + mfar_ref[:, 0:fl]
            (ef, ew), rl = _softmax2([tf, tw], [cfar_ref[h], None])
            o = (_dot(ef.astype(BF16), v_ref[0:fl, :]) + _dot(ew.astype(BF16), v_win)) * rl
        else:
            (ew,), rl = _softmax2([tw], [None])
            o = _dot(ew.astype(BF16), v_win) * rl
        o_ref[:, pl.ds(off, HEAD_DIM)] = o.astype(o_ref.dtype)
        return carry

    lax.fori_loop(0, N_HEADS, head, 0, unroll=HEAD_UNROLL)


def _dsa_prompt_body(buckets, topk, l_ref, *refs):
    i = pl.program_id(1)
    for lo, hi, fl in buckets:
        pl.when((i >= lo) & (i < hi))(functools.partial(_dsa_block, fl, topk, i, *refs))


def _dsa_prompt(lidx, projq, pkv_b, pkv_f, dwin, cfar, batch, seq, m_total, cols):
    nq = seq // QB
    far_max = seq - WIN
    topk = min(TOPK_MAX, seq // 4)
    width = N_HEADS * HEAD_DIM
    return pl.pallas_call(
        functools.partial(_dsa_prompt_body, _far_buckets(nq, far_max), topk),
        grid_spec=pltpu.PrefetchScalarGridSpec(
            num_scalar_prefetch=1, grid=(batch, nq),
            in_specs=[
                pl.BlockSpec(memory_space=pltpu.SMEM),
                pl.BlockSpec((QB, width), lambda b, i, l: (b * nq + i, cols["qa"])),
                pl.BlockSpec((QB, N_IDX_HEADS * IDX_DIM), lambda b, i, l: (b * nq + i, cols["qi"])),
                pl.BlockSpec((QB, LANES), lambda b, i, l: (b * nq + i, cols["wi"])),
                pl.BlockSpec((seq, HEAD_DIM), lambda b, i, l: (b, cols["ka"])),
                pl.BlockSpec((seq, HEAD_DIM), lambda b, i, l: (b, cols["va"])),
                pl.BlockSpec((seq, LANES), lambda b, i, l: (b, cols["kx"])),
                pl.BlockSpec((N_HEADS, QB, WIN), lambda b, i, l: (0, 0, 0)),
            ],
            out_specs=pl.BlockSpec((QB, width), lambda b, i, l: (b * nq + i, 0)),
            scratch_shapes=[pltpu.VMEM((QB, far_max), F32), pltpu.VMEM((QB, WIN), F32)]),
        out_shape=jax.ShapeDtypeStruct((m_total, width), BF16),
        compiler_params=_cparams("parallel", "arbitrary"),
        name="dsa_prompt",
    )(lidx, cfar, projq, projq, pkv_f, pkv_b, pkv_b, pkv_b, dwin)


def _diff_block(fl, i, l_ref, cfar_ref, laminit_ref, qb_ref, kb_ref, vb_ref, dwin_ref,
                lq1_ref, lk1_ref, lq2_ref, lk2_ref, g_ref, o_ref, mfar_ref, mwin_ref):
    prev0 = pl.multiple_of(jnp.maximum(i - 1, 0) * QB, QB)
    diag0 = pl.multiple_of(i * QB, QB)
    lam_init = laminit_ref[l_ref[0]]
    lam = _lambda(lq1_ref[...], lk1_ref[...], lq2_ref[...], lk2_ref[...], lam_init)
    col_win = lax.broadcasted_iota(jnp.int32, (QB, WIN), 1)
    mwin_ref[...] = jnp.where((col_win >= QB) | (i > 0), 0.0, NEG)
    if fl:
        col_far = lax.broadcasted_iota(jnp.int32, (QB, fl), 1)
        mfar_ref[:, 0:fl] = jnp.where(col_far < (i - 1) * QB, 0.0, NEG)
    c1 = SUB_DIM ** -0.5 * LOG2E
    g = g_ref[...]

    def head(h, carry):
        off = pl.multiple_of(h * HEAD_DIM, HEAD_DIM)
        hc = pl.ds(off, HEAD_DIM)
        q1, q2 = _lane_half_masks(qb_ref[:, hc])
        k_win = _window(kb_ref, prev0, diag0, hc)
        v_win = _window(vb_ref, prev0, diag0, hc)
        bias_win = dwin_ref[h] + mwin_ref[...]
        if fl:
            k_far = kb_ref[0:fl, hc]
            shifts = [cfar_ref[h], None]
            (e1f, e1w), r1 = _softmax2([_dot_nt(q1, k_far) * c1 + mfar_ref[:, 0:fl],
                                        _dot_nt(q1, k_win) * c1 + bias_win], shifts)
            (e2f, e2w), r2 = _softmax2([_dot_nt(q2, k_far) * c1 + mfar_ref[:, 0:fl],
                                        _dot_nt(q2, k_win) * c1 + bias_win], shifts)
            r2 = r2 * lam
            o = (_dot((e1f * r1 - e2f * r2).astype(BF16), vb_ref[0:fl, hc])
                 + _dot((e1w * r1 - e2w * r2).astype(BF16), v_win))
        else:
            (e1w,), r1 = _softmax2([_dot_nt(q1, k_win) * c1 + bias_win], [None])
            (e2w,), r2 = _softmax2([_dot_nt(q2, k_win) * c1 + bias_win], [None])
            r2 = r2 * lam
            o = _dot((e1w * r1 - e2w * r2).astype(BF16), v_win)
        o_ref[:, hc] = _sub_ln(o, g, lam_init).astype(o_ref.dtype)
        return carry

    lax.fori_loop(0, N_HEADS, head, 0, unroll=HEAD_UNROLL)


def _diff_prompt_body(buckets, l_ref, *refs):
    i = pl.program_id(1)
    for lo, hi, fl in buckets:
        pl.when((i >= lo) & (i < hi))(functools.partial(_diff_block, fl, i, l_ref, *refs))


def _diff_prompt(lidx, projq, pkv_b, dwin, cfar, lam_inits, lq1, lk1, lq2, lk2, subg, batch, seq, m_total, cols):
    nq = seq // QB
    far_max = seq - WIN
    width = N_HEADS * HEAD_DIM
    vec = lambda n: pl.BlockSpec((None, 1, n), lambda b, i, l: (l[0], 0, 0))
    return pl.pallas_call(
        functools.partial(_diff_prompt_body, _far_buckets(nq, far_max)),
        grid_spec=pltpu.PrefetchScalarGridSpec(
            num_scalar_prefetch=1, grid=(batch, nq),
            in_specs=[
                pl.BlockSpec(memory_space=pltpu.SMEM),
                pl.BlockSpec(memory_space=pltpu.SMEM),
                pl.BlockSpec((QB, width), lambda b, i, l: (b * nq + i, cols["qb"])),
                pl.BlockSpec((seq, width), lambda b, i, l: (b, cols["kb"])),
                pl.BlockSpec((seq, width), lambda b, i, l: (b, cols["vb"])),
                pl.BlockSpec((N_HEADS, QB, WIN), lambda b, i, l: (0, 0, 0)),
                vec(SUB_DIM), vec(SUB_DIM), vec(SUB_DIM), vec(SUB_DIM), vec(HEAD_DIM),
            ],
            out_specs=pl.BlockSpec((QB, width), lambda b, i, l: (b * nq + i, 0)),
            scratch_shapes=[pltpu.VMEM((QB, far_max), F32), pltpu.VMEM((QB, WIN), F32)]),
        out_shape=jax.ShapeDtypeStruct((m_total, width), BF16),
        compiler_params=_cparams("parallel", "arbitrary"),
        name="diff_prompt",
    )(lidx, cfar, lam_inits, projq, pkv_b, pkv_b, dwin, lq1, lk1, lq2, lk2, subg)


def _dsa_sample_body(topk, l_ref, qa_ref, qi_ref, wi_ref, kn_ref, vn_ref, kxn_ref, kc_ref, vc_ref, kxc_ref,
                     bc_ref, bn_ref, oin_ref, o_ref, mc_ref, mn_ref):
    s_c, s_n = _index_scores(qi_ref[...], wi_ref[...], [kxc_ref[...].astype(BF16), kxn_ref[...]])
    key_c = _sortable(s_c)
    key_n = _sortable(s_n)
    thr = _kth_largest([key_c, key_n], topk)
    mc_ref[...] = jnp.where(key_c >= thr, 0.0, NEG)
    mn_ref[...] = jnp.where(key_n >= thr, 0.0, NEG)
    k_c = kc_ref[...].astype(BF16)
    v_c = vc_ref[...].astype(BF16)
    k_n = kn_ref[...]
    v_n = vn_ref[...]
    c1 = HEAD_DIM ** -0.5 * LOG2E

    def head(h, carry):
        off = pl.multiple_of(h * HEAD_DIM, HEAD_DIM)
        q = qa_ref[:, pl.ds(off, HEAD_DIM)]
        tc = _dot_nt(q, k_c) * c1 + (bc_ref[h] + mc_ref[...])
        tn = _dot_nt(q, k_n) * c1 + (bn_ref[h] + mn_ref[...])
        (ec, en), rl = _softmax2([tc, tn], [None, None])
        o = (_dot(ec.astype(BF16), v_c) + _dot(en.astype(BF16), v_n)) * rl
        o_ref[:, pl.ds(off, HEAD_DIM)] = o.astype(o_ref.dtype)
        return carry

    lax.fori_loop(0, N_HEADS, head, 0)


def _dsa_sample(lidx, projq, pkv_b, pkv_f, cache_ka, cache_va, cache_kx2, bias_c, bias_n, o_all, row0, cols):
    _, nb, past, _ = cache_ka.shape
    t = bias_c.shape[1]
    r0 = row0 // t
    topk = min(TOPK_MAX, (past + t) // 4)
    width = N_HEADS * HEAD_DIM
    w128 = lambda key: (lambda b, l: (r0 + b, cols[key]))
    cache = lambda n: pl.BlockSpec((None, None, past, n), lambda b, l: (l[0], b, 0, 0))
    return pl.pallas_call(
        functools.partial(_dsa_sample_body, topk),
        grid_spec=pltpu.PrefetchScalarGridSpec(
            num_scalar_prefetch=1, grid=(nb,),
            in_specs=[
                pl.BlockSpec((t, width), w128("qa")),
                pl.BlockSpec((t, N_IDX_HEADS * IDX_DIM), w128("qi")),
                pl.BlockSpec((t, LANES), w128("wi")),
                pl.BlockSpec((t, HEAD_DIM), w128("ka")),
                pl.BlockSpec((t, HEAD_DIM), w128("va")),
                pl.BlockSpec((t, LANES), w128("kx")),
                cache(HEAD_DIM), cache(HEAD_DIM), cache(LANES),
                pl.BlockSpec((N_HEADS, t, past), lambda b, l: (0, 0, 0)),
                pl.BlockSpec((N_HEADS, t, t), lambda b, l: (0, 0, 0)),
                pl.BlockSpec(memory_space=pl.ANY),
            ],
            out_specs=pl.BlockSpec((t, width), lambda b, l: (r0 + b, 0)),
            scratch_shapes=[pltpu.VMEM((t, past), F32), pltpu.VMEM((t, t), F32)]),
        out_shape=jax.ShapeDtypeStruct(o_all.shape, BF16),
        input_output_aliases={12: 0},
        compiler_params=_cparams("arbitrary"),
        name="dsa_sample",
    )(lidx, projq, projq, pkv_f, pkv_b, pkv_b, pkv_b, cache_ka, cache_va, cache_kx2, bias_c, bias_n, o_all)


def _diff_sample_body(l_ref, laminit_ref, qb_ref, kn_ref, vn_ref, kc_ref, vc_ref, bc_ref, bn_ref,
                      lq1_ref, lk1_ref, lq2_ref, lk2_ref, g_ref, oin_ref, o_ref):
    lam_init = laminit_ref[l_ref[0]]
    lam = _lambda(lq1_ref[...], lk1_ref[...], lq2_ref[...], lk2_ref[...], lam_init)
    c1 = SUB_DIM ** -0.5 * LOG2E
    g = g_ref[...]

    def head(h, carry):
        off = pl.multiple_of(h * HEAD_DIM, HEAD_DIM)
        hc = pl.ds(off, HEAD_DIM)
        q1, q2 = _lane_half_masks(qb_ref[:, hc])
        k_c = kc_ref[:, hc].astype(BF16)
        v_c = vc_ref[:, hc].astype(BF16)
        k_n = kn_ref[:, hc]
        v_n = vn_ref[:, hc]
        bc = bc_ref[h]
        bn = bn_ref[h]
        none2 = [None, None]
        (e1c, e1n), r1 = _softmax2([_dot_nt(q1, k_c) * c1 + bc, _dot_nt(q1, k_n) * c1 + bn], none2)
        (e2c, e2n), r2 = _softmax2([_dot_nt(q2, k_c) * c1 + bc, _dot_nt(q2, k_n) * c1 + bn], none2)
        r2 = r2 * lam
        o = (_dot((e1c * r1 - e2c * r2).astype(BF16), v_c) + _dot((e1n * r1 - e2n * r2).astype(BF16), v_n))
        o_ref[:, hc] = _sub_ln(o, g, lam_init).astype(o_ref.dtype)
        return carry

    lax.fori_loop(0, N_HEADS, head, 0)


def _diff_sample(lidx, projq, pkv_b, cache_kb, cache_vb, bias_c, bias_n, lam_inits, lq1, lk1, lq2, lk2, subg,
                 o_all, row0, cols):
    _, nb, past, width = cache_kb.shape
    t = bias_c.shape[1]
    r0 = row0 // t
    vec = lambda n: pl.BlockSpec((None, 1, n), lambda b, l: (l[0], 0, 0))
    cache = pl.BlockSpec((None, None, past, width), lambda b, l: (l[0], b, 0, 0))
    return pl.pallas_call(
        _diff_sample_body,
        grid_spec=pltpu.PrefetchScalarGridSpec(
            num_scalar_prefetch=1, grid=(nb,),
            in_specs=[
                pl.BlockSpec(memory_space=pltpu.SMEM),
                pl.BlockSpec((t, width), lambda b, l: (r0 + b, cols["qb"])),
                pl.BlockSpec((t, width), lambda b, l: (r0 + b, cols["kb"])),
                pl.BlockSpec((t, width), lambda b, l: (r0 + b, cols["vb"])),
                cache, cache,
                pl.BlockSpec((N_HEADS, t, past), lambda b, l: (0, 0, 0)),
                pl.BlockSpec((N_HEADS, t, t), lambda b, l: (0, 0, 0)),
                vec(SUB_DIM), vec(SUB_DIM), vec(SUB_DIM), vec(SUB_DIM), vec(HEAD_DIM),
                pl.BlockSpec(memory_space=pl.ANY),
            ],
            out_specs=pl.BlockSpec((t, width), lambda b, l: (r0 + b, 0))),
        out_shape=jax.ShapeDtypeStruct(o_all.shape, BF16),
        input_output_aliases={14: 0},
        compiler_params=_cparams("arbitrary"),
        name="diff_sample",
    )(lidx, lam_inits, projq, pkv_b, pkv_b, cache_kb, cache_vb, bias_c, bias_n, lq1, lk1, lq2, lk2, subg, o_all)


def _layer_norm_rows(z, g, b):
    mu = jnp.mean(z, axis=-1, keepdims=True)
    zc = z - mu
    var = jnp.mean(zc * zc, axis=-1, keepdims=True)
    return zc * lax.rsqrt(var + LN_EPS) * g + b


def _route(rl):
    lane = lax.broadcasted_iota(jnp.int32, rl.shape, 1).astype(F32)
    big = 1e6
    gl = jnp.where(lane < N_GROUPS, rl, NEG)
    gmax = gl.max(axis=-1, keepdims=True)
    gsum = jnp.sum(jnp.exp(gl - gmax), axis=-1, keepdims=True)
    g_val = 1.0 / gsum
    g_idx = jnp.min(jnp.where(gl == gmax, lane, big), axis=-1, keepdims=True)
    lo = N_GROUPS + g_idx * EXPERTS_PER_GROUP
    el = jnp.where((lane >= lo) & (lane < lo + EXPERTS_PER_GROUP), rl, NEG)
    m1 = el.max(axis=-1, keepdims=True)
    i1 = jnp.min(jnp.where(el == m1, lane, big), axis=-1, keepdims=True)
    el2 = jnp.where(lane == i1, NEG, el)
    m2 = el2.max(axis=-1, keepdims=True)
    i2 = jnp.min(jnp.where(el2 == m2, lane, big), axis=-1, keepdims=True)
    e2 = jnp.exp(m2 - m1)
    den = 1.0 / (1.0 + e2)
    w1 = den * g_val
    w2 = e2 * den * g_val
    out = jnp.where(lane == 0, i1 - N_GROUPS, 0.0)
    out = jnp.where(lane == 1, i2 - N_GROUPS, out)
    out = jnp.where(lane == 2, w1, out)
    out = jnp.where(lane == 3, w2, out)
    return out


HI16 = -65536


def _pack_bf16_pair(x):
    n = x.shape[1] // 2
    bits = pltpu.bitcast(x.astype(BF16).astype(F32), jnp.int32)
    return lax.shift_right_logical(bits[:, :n], 16) | (bits[:, n:] & HI16)


def _unpack_bf16_pair(p):
    return pltpu.bitcast(lax.shift_left(p, 16), F32), pltpu.bitcast(p & HI16, F32)


def _out_ln_body(alpha, tn, rows, l_ref, oa_ref, ob_ref, wa_ref, wb_ref, x_ref, g_ref, b_ref, wr_ref,
                 h_ref, hp_ref, rt_ref):
    j = pl.program_id(1)
    acc = _dot(oa_ref[...], wa_ref[...]) + _dot(ob_ref[...], wb_ref[...])
    off = pl.multiple_of(j * tn, tn)
    h_ref[:, pl.ds(off, tn)] = alpha * x_ref[...] + acc

    @pl.when(j == pl.num_programs(1) - 1)
    def _():
        g = g_ref[...]
        b = b_ref[...]

        def chunk(c, carry):
            r = pl.multiple_of(c * rows, rows)
            hn = _layer_norm_rows(h_ref[pl.ds(r, rows), :], g, b)
            h_ref[pl.ds(r, rows), :] = hn
            hp_ref[pl.ds(r, rows), :] = _pack_bf16_pair(hn)
            rt_ref[pl.ds(r, rows), :] = _dot(hn.astype(BF16), wr_ref[...])
            return carry

        lax.fori_loop(0, h_ref.shape[0] // rows, chunk, 0)
        rt_ref[...] = _route(rt_ref[...])


def _out_ln(lidx, oa, ob, w_out, x, g, b, wr, alpha, tm, tn):
    m, half = oa.shape
    d = w_out.shape[-1]
    rows = next(r for r in (64, 48, 32, 16, 8) if tm % r == 0)
    return pl.pallas_call(
        functools.partial(_out_ln_body, alpha, tn, rows),
        grid_spec=pltpu.PrefetchScalarGridSpec(
            num_scalar_prefetch=1, grid=(m // tm, d // tn),
            in_specs=[
                pl.BlockSpec((tm, half), lambda i, j, l: (i, 0)),
                pl.BlockSpec((tm, half), lambda i, j, l: (i, 0)),
                pl.BlockSpec((None, half, tn), lambda i, j, l: (l[0], 0, j)),
                pl.BlockSpec((None, half, tn), lambda i, j, l: (l[0], 1, j)),
                pl.BlockSpec((tm, tn), lambda i, j, l: (i, j)),
                pl.BlockSpec((None, 1, d), lambda i, j, l: (l[0], 0, 0)),
                pl.BlockSpec((None, 1, d), lambda i, j, l: (l[0], 0, 0)),
                pl.BlockSpec((None, d, LANES), lambda i, j, l: (l[0], 0, 0)),
            ],
            out_specs=[pl.BlockSpec((tm, d), lambda i, j, l: (i, 0)),
                       pl.BlockSpec((tm, d // 2), lambda i, j, l: (i, 0)),
                       pl.BlockSpec((tm, LANES), lambda i, j, l: (i, 0))]),
        out_shape=[jax.ShapeDtypeStruct((m, d), F32), jax.ShapeDtypeStruct((m, d // 2), jnp.int32),
                   jax.ShapeDtypeStruct((m, LANES), F32)],
        compiler_params=_cparams("parallel", "arbitrary"),
        name="out_proj_ln_route",
    )(lidx, oa, ob, w_out, w_out, x, g, b, wr)


def _moe_body(tm, spare_row, meta_ref, te_ref, src_ref, nx1_ref, nx2_ref, dst_ref, prv_ref, gate_ref, h_hbm,
              w1_ref, w3_ref, w2_ref, y_hbm, xbuf, obuf, xb, gsem, ssem):
    t = pl.program_id(0)
    n_used = meta_ref[1]
    slot = lax.rem(t, 2)
    other = 1 - slot
    g0 = lax.rem(t, 3)
    g1 = lax.rem(t + 1, 3)
    g2 = lax.rem(t + 2, 3)

    def wait_gather(s):
        pltpu.make_async_copy(h_hbm.at[pl.ds(0, tm)], xbuf.at[s], gsem.at[s]).wait()

    def wait_scatter(s):
        pltpu.make_async_copy(obuf.at[s], y_hbm.at[pl.ds(0, tm)], ssem.at[s]).wait()

    @pl.when(t == 0)
    def _():
        def row(r, carry):
            pltpu.make_async_copy(h_hbm.at[pl.ds(src_ref[0, r], 1)], xbuf.at[0, pl.ds(r, 1)], gsem.at[0]).start()
            pltpu.make_async_copy(h_hbm.at[pl.ds(nx1_ref[0, r], 1)], xbuf.at[1, pl.ds(r, 1)], gsem.at[1]).start()
            return carry
        lax.fori_loop(0, tm, row, 0, unroll=8)
        obuf[...] = jnp.zeros_like(obuf)
        pltpu.make_async_copy(obuf.at[0], y_hbm.at[pl.ds(spare_row, tm)], ssem.at[0]).start()

    @pl.when(t < n_used)
    def _():
        wait_gather(g0)
        wait_scatter(slot)
        lo, hi = _unpack_bf16_pair(xbuf[g0])
        half = lo.shape[1]
        xb[:, 0:half] = lo.astype(BF16)
        xb[:, half:2 * half] = hi.astype(BF16)
        for r in range(tm):
            pltpu.make_async_copy(h_hbm.at[pl.ds(nx2_ref[0, r], 1)], xbuf.at[g2, pl.ds(r, 1)],
                                  gsem.at[g2]).start()
        for r in range(tm):
            pltpu.make_async_copy(obuf.at[other, pl.ds(r, 1)], y_hbm.at[pl.ds(prv_ref[0, r], 1)],
                                  ssem.at[other]).start()
        x = xb[...]
        u = _dot(x, w1_ref[...])
        v = _dot(x, w3_ref[...])
        a = (u * (1.0 / (1.0 + jnp.exp(-u))) * v).astype(BF16)
        obuf[slot] = _pack_bf16_pair(_dot(a, w2_ref[...]) * gate_ref[:, 0:1])

        @pl.when(t + 1 >= n_used)
        def _():
            def row(r, carry):
                pltpu.make_async_copy(obuf.at[slot, pl.ds(r, 1)], y_hbm.at[pl.ds(dst_ref[0, r], 1)],
                                      ssem.at[slot]).start()
                return carry
            lax.fori_loop(0, tm, row, 0, unroll=8)
            wait_gather(g1)
            wait_gather(g2)
            wait_scatter(other)
            wait_scatter(slot)


def _moe(meta, tile_expert, src3, dst3, prv3, gate, hp, w1, w3, w2, tm, y_rows):
    d, f = w1.shape[-2:]
    dp = hp.shape[1]
    nt = src3.shape[0]
    spare_row = y_rows - tm
    idx_spec = lambda imap: pl.BlockSpec((None, 1, tm), imap, memory_space=pltpu.SMEM)
    wspec = lambda shape, nbuf: pl.BlockSpec((None, None) + shape, lambda t, meta, te: (meta[0], te[t], 0, 0),
                                             pipeline_mode=pl.Buffered(nbuf))
    return pl.pallas_call(
        functools.partial(_moe_body, tm, spare_row),
        grid_spec=pltpu.PrefetchScalarGridSpec(
            num_scalar_prefetch=2, grid=(nt,),
            in_specs=[
                idx_spec(lambda t, meta, te: (t, 0, 0)),
                idx_spec(lambda t, meta, te: (jnp.minimum(t + 1, nt - 1), 0, 0)),
                idx_spec(lambda t, meta, te: (jnp.minimum(t + 2, nt - 1), 0, 0)),
                idx_spec(lambda t, meta, te: (t, 0, 0)),
                idx_spec(lambda t, meta, te: (t, 0, 0)),
                pl.BlockSpec((tm, LANES), lambda t, meta, te: (t, 0)),
                pl.BlockSpec(memory_space=pl.ANY),
                wspec((d, f), 1), wspec((d, f), 1), wspec((f, d), 2),
            ],
            out_specs=pl.BlockSpec(memory_space=pl.ANY),
            scratch_shapes=[pltpu.VMEM((3, tm, dp), jnp.int32), pltpu.VMEM((2, tm, dp), jnp.int32),
                            pltpu.VMEM((tm, d), BF16),
                            pltpu.SemaphoreType.DMA((3,)), pltpu.SemaphoreType.DMA((2,))]),
        out_shape=jax.ShapeDtypeStruct((y_rows, dp), jnp.int32),
        compiler_params=_cparams("arbitrary"),
        name="moe_experts",
    )(meta, tile_expert, src3, src3, src3, dst3, prv3, gate, hp, w1, w3, w2)


def _dispatch_plan(route, tm):
    m = route.shape[0]
    npairs = 2 * m
    nt = npairs // tm + N_EXPERTS
    p = nt * tm
    e = route[:, 0:2].astype(jnp.int32).reshape(npairs)
    wgt = route[:, 2:4].reshape(npairs)
    order = jnp.argsort(e, stable=True).astype(jnp.int32)
    counts = jnp.sum((e[:, None] == jnp.arange(N_EXPERTS, dtype=jnp.int32)[None, :]).astype(jnp.int32), axis=0)
    padded = ((counts + tm - 1) // tm) * tm
    pad_end = jnp.cumsum(padded)
    pad_start = pad_end - padded
    cnt_start = jnp.cumsum(counts) - counts
    tile_start = jnp.arange(nt, dtype=jnp.int32) * tm
    tile_expert = jnp.minimum(jnp.sum((pad_end[None, :] <= tile_start[:, None]).astype(jnp.int32), axis=1),
                              N_EXPERTS - 1)
    slot = jnp.arange(p, dtype=jnp.int32)
    se = jnp.repeat(tile_expert, tm)
    rank = slot - pad_start[se]
    valid = (rank >= 0) & (rank < counts[se])
    pair = order[jnp.clip(cnt_start[se] + rank, 0, npairs - 1)]
    src = jnp.where(valid, pair // 2, 0)
    dst = jnp.where(valid, (pair % 2) * m + pair // 2, npairs + slot)
    gate = jnp.where(valid, wgt[pair], 0.0)
    n_used = (pad_end[-1] // tm).astype(jnp.int32)
    gate128 = jnp.broadcast_to(gate[:, None], (p, LANES))
    dst3 = dst.reshape(nt, 1, tm)
    spare = (npairs + p + jnp.arange(tm, dtype=jnp.int32)).reshape(1, 1, tm)
    prv3 = jnp.concatenate([spare, dst3[:-1]], axis=0)
    return src.reshape(nt, 1, tm), dst3, prv3, gate128, tile_expert.astype(jnp.int32), n_used


def _ln2_body(alpha, l_ref, h_ref, y0_ref, y1_ref, g_ref, b_ref, o_ref, ob_ref):
    lo0, hi0 = _unpack_bf16_pair(y0_ref[...])
    lo1, hi1 = _unpack_bf16_pair(y1_ref[...])
    z = alpha * h_ref[...] + jnp.concatenate([lo0 + lo1, hi0 + hi1], axis=1)
    hn = _layer_norm_rows(z, g_ref[...], b_ref[...])
    o_ref[...] = hn
    ob_ref[...] = hn.astype(BF16)


def _ln2(lidx, h, y, g, b, alpha, tm):
    m, d = h.shape
    nblk = m // tm
    return pl.pallas_call(
        functools.partial(_ln2_body, alpha),
        grid_spec=pltpu.PrefetchScalarGridSpec(
            num_scalar_prefetch=1, grid=(nblk,),
            in_specs=[
                pl.BlockSpec((tm, d), lambda i, l: (i, 0)),
                pl.BlockSpec((tm, d // 2), lambda i, l: (i, 0)),
                pl.BlockSpec((tm, d // 2), lambda i, l: (i + nblk, 0)),
                pl.BlockSpec((None, 1, d), lambda i, l: (l[0], 0, 0)),
                pl.BlockSpec((None, 1, d), lambda i, l: (l[0], 0, 0)),
            ],
            out_specs=[pl.BlockSpec((tm, d), lambda i, l: (i, 0)),
                       pl.BlockSpec((tm, d), lambda i, l: (i, 0))]),
        out_shape=[jax.ShapeDtypeStruct((m, d), F32), jax.ShapeDtypeStruct((m, d), BF16)],
        compiler_params=_cparams("parallel"),
        name="moe_combine_ln",
    )(lidx, h, y, y, g, b)


def _pick_tile(m, candidates):
    for c in candidates:
        if m % c == 0:
            return c
    raise ValueError(f"no row tile for {m} rows")


def kernel(x_prompt, x_sample, cache_ka, cache_va, cache_kidx, cache_kb, cache_vb, rel_bias, w_in, w_out,
           lambda_q1, lambda_k1, lambda_q2, lambda_k2, subln_g, ln1_g, ln1_b, ln2_g, ln2_b,
           w_route_group, w_route_expert, w1, w3, w2):
    batch, seq, d = x_prompt.shape
    dec_batch, dec_seq, _ = x_sample.shape
    depth = w_in.shape[0]
    past = cache_ka.shape[2]
    mp = batch * seq
    ms = dec_batch * dec_seq
    m = mp + ms
    alpha = (2.0 * depth) ** 0.25
    width = N_HEADS * HEAD_DIM
    assert seq % QB == 0 and seq >= 2 * WIN and mp % dec_seq == 0 and width == 2048

    in_widths = (width, HEAD_DIM, HEAD_DIM, N_IDX_HEADS * IDX_DIM, IDX_DIM, N_IDX_HEADS, width, width, width)
    splits = [int(s) for s in np.cumsum(in_widths)[:-1]]
    wqa, wka, wva, wqi, wki, wwi, wqb, wkb, wvb = jnp.split(w_in.astype(BF16), splits, axis=-1)
    w_q = jnp.concatenate([wqa, wqb, wqi], axis=-1)
    pad = jnp.zeros(wwi.shape[:-1] + (LANES - N_IDX_HEADS,), BF16)
    w_kv = jnp.concatenate([wkb, wvb, wka, wva, wki, wki, wwi, pad], axis=-1)
    n_kv = w_kv.shape[-1]
    cols_q = {"qa": 0, "qb": 1, "qi": 4}
    cols_kv = {"kb": 0, "vb": 1, "ka": 32, "va": 33, "kx": 34, "wi": 35}
    cols = {**cols_q, **cols_kv}
    w_out_b = w_out.astype(BF16)
    w1_b = w1.astype(BF16)
    w3_b = w3.astype(BF16)
    w2_b = w2.astype(BF16)
    w_r = jnp.concatenate([w_route_group, w_route_expert,
                           jnp.zeros((depth, d, LANES - N_GROUPS - N_EXPERTS), F32)], axis=-1).astype(BF16)
    vec3 = lambda a: a.reshape(depth, 1, a.shape[-1])
    lq1, lk1, lq2, lk2, subg = map(vec3, (lambda_q1, lambda_k1, lambda_q2, lambda_k2, subln_g))
    g1, b1, g2, b2 = map(vec3, (ln1_g, ln1_b, ln2_g, ln2_b))
    lam_inits = jnp.asarray([0.8 - 0.6 * math.exp(-0.3 * l) for l in range(depth)], F32)

    dwin, cfar = _prompt_bias_tiles(rel_bias)
    sbias = _sample_bias(rel_bias, past, dec_seq)
    dwin_a, dwin_b = dwin[:N_HEADS], dwin[N_HEADS:]
    cfar_a, cfar_b = cfar[:N_HEADS], cfar[N_HEADS:]
    sb_a_c, sb_a_n = sbias[:N_HEADS, :, :past], sbias[:N_HEADS, :, past:]
    sb_b_c, sb_b_n = sbias[N_HEADS:, :, :past], sbias[N_HEADS:, :, past:]
    cache_kb2 = cache_kb.astype(BF16).reshape(cache_kb.shape[:3] + (width,))
    cache_vb2 = cache_vb.astype(BF16).reshape(cache_vb.shape[:3] + (width,))
    cache_kx2 = jnp.concatenate([cache_kidx, cache_kidx], axis=-1)

    tm_proj = _pick_tile(m, (768, 512, 256, 128, 64, 32))
    tm_out = _pick_tile(m, (528, 384, 256, 128, 64, 32))
    tm_ln2 = _pick_tile(m, (256, 128, 64, 32))
    tm_moe = 256 if m >= 1024 else 32
    y_rows = 2 * m + (2 * m // tm_moe + N_EXPERTS) * tm_moe + 2 * tm_moe

    x0 = jnp.concatenate([x_prompt.reshape(mp, d), x_sample.reshape(ms, d)], axis=0)
    x, xb = x0, x0.astype(BF16)
    rows = []
    for layer in range(depth):
        lidx = jnp.full((1,), layer, jnp.int32)
        projq = _proj(lidx, xb, w_q, tm_proj, 512, False)[0]
        pkv_b, pkv_f = _proj(lidx, xb, w_kv, tm_proj, 512, True)
        oa = _dsa_prompt(lidx, projq, pkv_b, pkv_f, dwin_a, cfar_a, batch, seq, m, cols)
        ob = _diff_prompt(lidx, projq, pkv_b, dwin_b, cfar_b, lam_inits, lq1, lk1, lq2, lk2, subg,
                          batch, seq, m, cols)
        oa = _dsa_sample(lidx, projq, pkv_b, pkv_f, cache_ka, cache_va, cache_kx2, sb_a_c, sb_a_n, oa, mp, cols)
        ob = _diff_sample(lidx, projq, pkv_b, cache_kb2, cache_vb2, sb_b_c, sb_b_n, lam_inits,
                          lq1, lk1, lq2, lk2, subg, ob, mp, cols)
        h1, h1p, route = _out_ln(lidx, oa, ob, w_out_b, x, g1, b1, w_r, alpha, tm_out, 512)
        src3, dst3, prv3, gate128, tile_expert, n_used = _dispatch_plan(route, tm_moe)
        meta = jnp.stack([lidx[0], n_used])
        y = _moe(meta, tile_expert, src3, dst3, prv3, gate128, h1p, w1_b, w3_b, w2_b, tm_moe, y_rows)
        x, xb = _ln2(lidx, h1, y, g2, b2, alpha, tm_ln2)
        rows.append(pkv_f)
    y = x
    heads = (N_HEADS, 2 * SUB_DIM)

    def split(c0, n, tail):
        return (jnp.stack([r[:mp, c0:c0 + n] for r in rows]).reshape((depth, batch, seq) + tail),
                jnp.stack([r[mp:, c0:c0 + n] for r in rows]).reshape((depth, dec_batch, dec_seq) + tail))

    p_kb, s_kb = split(0, width, heads)
    p_vb, s_vb = split(width, width, heads)
    p_ka, s_ka = split(2 * width, HEAD_DIM, (HEAD_DIM,))
    p_va, s_va = split(2 * width + HEAD_DIM, HEAD_DIM, (HEAD_DIM,))
    p_kx, s_kx = split(2 * width + 2 * HEAD_DIM, IDX_DIM, (IDX_DIM,))
    y_prompt = y[:mp].reshape(batch, seq, d)
    y_sample = y[mp:].reshape(dec_batch, dec_seq, d)
    return (y_prompt, y_sample, p_ka, p_va, p_kx, p_kb, p_vb, s_ka, s_va, s_kx, s_kb, s_vb)
```

```python
import functools
import math

import numpy as np
import jax
import jax.numpy as jnp
from jax import lax
from jax.experimental import pallas as pl
from jax.experimental.pallas import tpu as pltpu

CHUNK = 64
HEAD_DIM = 128
N_HEADS = 16
N_IDX_HEADS = 16
IDX_DIM = 64
SUB_DIM = 64
TOPK_MAX = 256
NUM_BUCKETS = 32
MAX_DISTANCE = 128
N_GROUPS = 4
EXPERTS_PER_GROUP = 4
N_EXPERTS = 16
LN_EPS = 1e-5
RMS_EPS = 1e-5

LANES = 128
QB = 128
WIN = 2 * QB
FAR_STEP = 512
HEAD_UNROLL = 8
NEG = -1e30
INT_MIN = -2147483648
LOG2E = 1.4426950408889634
VMEM_LIMIT = 56 * 1024 * 1024

F32 = jnp.float32
BF16 = jnp.bfloat16


def _cparams(*sem):
    return pltpu.CompilerParams(dimension_semantics=sem, vmem_limit_bytes=VMEM_LIMIT)


def _dot_nt(a, b):
    return lax.dot_general(a, b, (((1,), (1,)), ((), ())), preferred_element_type=F32)


def _dot(a, b):
    return jnp.dot(a, b, preferred_element_type=F32)


def _bucket_np(rel):
    nb = NUM_BUCKETS // 2
    ret = np.where(rel > 0, nb, 0)
    n = np.abs(rel)
    max_exact = nb // 2
    nf = np.maximum(n, 1).astype(np.float32)
    frac = np.log(nf / np.float32(max_exact)) / np.float32(math.log(MAX_DISTANCE / max_exact))
    large = max_exact + (frac.astype(np.float32) * np.float32(nb - max_exact)).astype(np.int32)
    large = np.minimum(large, nb - 1)
    return (ret + np.where(n < max_exact, n, large)).astype(np.int32)


def _bias_lookup(rel_bias, bucket):
    onehot = (jnp.asarray(bucket.reshape(-1, 1)) == jnp.arange(NUM_BUCKETS, dtype=jnp.int32)[None, :]).astype(F32)
    vals = jnp.dot(onehot, rel_bias.astype(F32), precision=lax.Precision.HIGHEST)
    return jnp.transpose(vals).reshape((rel_bias.shape[1],) + bucket.shape)


def _prompt_bias_tiles(rel_bias):
    ql = np.arange(QB)[:, None]
    c = np.arange(WIN)[None, :]
    rel = c - QB - ql
    kl = c - QB
    adm = (c < QB) | ((kl // CHUNK) <= (ql // CHUNK))
    tiles = _bias_lookup(rel_bias, _bucket_np(rel)) * LOG2E
    tiles = jnp.where(jnp.asarray(adm)[None], tiles, NEG)
    far = rel_bias[int(_bucket_np(np.array([-(QB + 1)]))[0])] * LOG2E
    return tiles.astype(F32), far.astype(F32)


def _sample_bias(rel_bias, past_len, dec_seq):
    q = past_len + np.arange(dec_seq)[:, None]
    k = np.arange(past_len + dec_seq)[None, :]
    return (_bias_lookup(rel_bias, _bucket_np(k - q)) * LOG2E).astype(F32)


def _proj_body(l_ref, x_ref, w_ref, ob_ref, *of_ref):
    acc = _dot(x_ref[...], w_ref[...])
    ob_ref[...] = acc.astype(BF16)
    if of_ref:
        of_ref[0][...] = acc


def _proj(lidx, xb, w, tm, tn, with_f32):
    m, d = xb.shape
    n = w.shape[-1]
    out_shape = [jax.ShapeDtypeStruct((m, n), BF16)]
    out_specs = [pl.BlockSpec((tm, tn), lambda i, j, l: (i, j))]
    if with_f32:
        out_shape.append(jax.ShapeDtypeStruct((m, n), F32))
        out_specs.append(pl.BlockSpec((tm, tn), lambda i, j, l: (i, j)))
    return pl.pallas_call(
        _proj_body,
        grid_spec=pltpu.PrefetchScalarGridSpec(
            num_scalar_prefetch=1, grid=(m // tm, n // tn),
            in_specs=[pl.BlockSpec((tm, d), lambda i, j, l: (i, 0)),
                      pl.BlockSpec((None, d, tn), lambda i, j, l: (l[0], 0, j))],
            out_specs=out_specs),
        out_shape=out_shape,
        compiler_params=_cparams("parallel", "arbitrary"),
        name="in_proj_rows" if with_f32 else "in_proj",
    )(lidx, xb, w)


def _sortable(score):
    bits = pltpu.bitcast(score + 0.0, jnp.int32)
    return jnp.where(bits < 0, bits ^ jnp.int32(0x7FFFFFFF), bits)


def _kth_largest(keys, k):
    t = keys[0].shape[0]

    def count_ge(cand):
        cnt = jnp.zeros((t, 1), F32)
        for kk in keys:
            cnt = cnt + jnp.sum(jnp.where(kk >= cand, 1.0, 0.0), axis=-1, keepdims=True)
        return cnt

    def step(it, thr):
        hi = jnp.left_shift(jnp.int32(1), 31 - 2 * it)
        lo = jnp.left_shift(jnp.int32(1), 30 - 2 * it)
        c1 = thr + lo
        c2 = thr + hi
        c3 = c2 + lo
        n1, n2, n3 = count_ge(c1), count_ge(c2), count_ge(c3)
        return jnp.where(n3 >= k, c3, jnp.where(n2 >= k, c2, jnp.where(n1 >= k, c1, thr)))

    return lax.fori_loop(0, 16, step, jnp.full((t, 1), INT_MIN, jnp.int32))


def _index_scores(qi, wi, kparts):
    t = qi.shape[0]
    lane = lax.broadcasted_iota(jnp.int32, (t, LANES), 1)
    w = wi * (N_IDX_HEADS ** -0.5 * IDX_DIM ** -0.5)
    acc = [jnp.zeros((t, kp.shape[0]), F32) for kp in kparts]
    for p in range(N_IDX_HEADS // 2):
        qp = qi[:, p * LANES:(p + 1) * LANES]
        halves = (jnp.where(lane < IDX_DIM, qp, jnp.zeros_like(qp)),
                  jnp.where(lane >= IDX_DIM, qp, jnp.zeros_like(qp)))
        for s, qh in enumerate(halves):
            h = 2 * p + s
            wh = w[:, h:h + 1]
            for j, kp in enumerate(kparts):
                acc[j] = acc[j] + jnp.maximum(_dot_nt(qh, kp), 0.0) * wh
    return acc


def _softmax2(parts, shifts):
    m = None
    for tp, c in zip(parts, shifts):
        mj = tp.max(axis=-1, keepdims=True)
        if c is not None:
            mj = mj + c
        m = mj if m is None else jnp.maximum(m, mj)
    es, l = [], None
    for tp, c in zip(parts, shifts):
        e = jnp.exp2(tp - (m if c is None else m - c))
        es.append(e)
        lj = e.sum(axis=-1, keepdims=True)
        l = lj if l is None else l + lj
    return es, 1.0 / l


def _lane_half_masks(q):
    lane = lax.broadcasted_iota(jnp.int32, q.shape, 1)
    z = jnp.zeros_like(q)
    return jnp.where(lane < SUB_DIM, q, z), jnp.where(lane >= SUB_DIM, q, z)


def _lambda(lq1, lk1, lq2, lk2, lam_init):
    return (jnp.exp(jnp.sum(lq1 * lk1, axis=-1, keepdims=True))
            - jnp.exp(jnp.sum(lq2 * lk2, axis=-1, keepdims=True)) + lam_init)


def _sub_ln(o, g, lam_init):
    return o * lax.rsqrt(jnp.mean(o * o, axis=-1, keepdims=True) + RMS_EPS) * g * (1.0 - lam_init)


def _far_buckets(nq, far_max):
    lens = list(range(0, far_max, FAR_STEP)) + [far_max]
    if len(lens) > 2 and lens[-1] - lens[-2] < FAR_STEP:
        del lens[-2]
    out, lo = [], 0
    for fl in lens:
        hi = min(nq, fl // QB + 2)
        if hi > lo:
            out.append((lo, hi, fl))
            lo = hi
    assert lo == nq
    return out


def _window(ref, prev0, diag0, cols=slice(None)):
    return jnp.concatenate([ref[pl.ds(prev0, QB), cols], ref[pl.ds(diag0, QB), cols]], axis=0)


def _dsa_block(fl, topk, i, cfar_ref, qa_ref, qi_ref, wi_ref, k_ref, v_ref, kx_ref, dwin_ref,
               o_ref, mfar_ref, mwin_ref):
    prev0 = pl.multiple_of(jnp.maximum(i - 1, 0) * QB, QB)
    diag0 = pl.multiple_of(i * QB, QB)

    kparts = ([kx_ref[0:fl, :]] if fl else []) + [_window(kx_ref, prev0, diag0)]
    scores = _index_scores(qi_ref[...], wi_ref[...], kparts)
    s_win = scores[-1]
    col_win = lax.broadcasted_iota(jnp.int32, s_win.shape, 1)
    row_win = lax.broadcasted_iota(jnp.int32, s_win.shape, 0)
    ok_win = ((col_win < QB + CHUNK) | (row_win >= CHUNK)) & ((col_win >= QB) | (i > 0))
    key_win = jnp.where(ok_win, _sortable(s_win), INT_MIN)
    keys = [key_win]
    if fl:
        s_far = scores[0]
        col_far = lax.broadcasted_iota(jnp.int32, s_far.shape, 1)
        ok_far = col_far < (i - 1) * QB
        key_far = jnp.where(ok_far, _sortable(s_far), INT_MIN)
        keys = [key_far, key_win]
    thr = _kth_largest(keys, topk)
    mwin_ref[...] = jnp.where(ok_win & (key_win >= thr), 0.0, NEG)
    if fl:
        mfar_ref[:, 0:fl] = jnp.where(ok_far & (key_far >= thr), 0.0, NEG)

    k_win = _window(k_ref, prev0, diag0)
    v_win = _window(v_ref, prev0, diag0)
    c1 = HEAD_DIM ** -0.5 * LOG2E

    def head(h, carry):
        off = pl.multiple_of(h * HEAD_DIM, HEAD_DIM)
        q = qa_ref[:, pl.ds(off, HEAD_DIM)]
        tw = _dot_nt(q, k_win) * c1 + (dwin_ref[h] + mwin_ref[...])
        if fl:
            tf = _dot_nt(q, k_ref[0:fl, :]) * c1 + mfar_ref[:, 0:fl]
            (ef, ew), rl = _softmax2([tf, tw], [cfar_ref[h], None])
            o = (_dot(ef.astype(BF16), v_ref[0:fl, :]) + _dot(ew.astype(BF16), v_win)) * rl
        else:
            (ew,), rl = _softmax2([tw], [None])
            o = _dot(ew.astype(BF16), v_win) * rl
        o_ref[:, pl.ds(off, HEAD_DIM)] = o.astype(o_ref.dtype)
        return carry

    lax.fori_loop(0, N_HEADS, head, 0, unroll=HEAD_UNROLL)


def _dsa_prompt_body(buckets, topk, l_ref, *refs):
    i = pl.program_id(1)
    for lo, hi, fl in buckets:
        pl.when((i >= lo) & (i < hi))(functools.partial(_dsa_block, fl, topk, i, *refs))


def _dsa_prompt(lidx, projq, pkv_b, pkv_f, dwin, cfar, batch, seq, m_total, cols):
    nq = seq // QB
    far_max = seq - WIN
    topk = min(TOPK_MAX, seq // 4)
    width = N_HEADS * HEAD_DIM
    return pl.pallas_call(
        functools.partial(_dsa_prompt_body, _far_buckets(nq, far_max), topk),
        grid_spec=pltpu.PrefetchScalarGridSpec(
            num_scalar_prefetch=1, grid=(batch, nq),
            in_specs=[
                pl.BlockSpec(memory_space=pltpu.SMEM),
                pl.BlockSpec((QB, width), lambda b, i, l: (b * nq + i, cols["qa"])),
                pl.BlockSpec((QB, N_IDX_HEADS * IDX_DIM), lambda b, i, l: (b * nq + i, cols["qi"])),
                pl.BlockSpec((QB, LANES), lambda b, i, l: (b * nq + i, cols["wi"])),
                pl.BlockSpec((seq, HEAD_DIM), lambda b, i, l: (b, cols["ka"])),
                pl.BlockSpec((seq, HEAD_DIM), lambda b, i, l: (b, cols["va"])),
                pl.BlockSpec((seq, LANES), lambda b, i, l: (b, cols["kx"])),
                pl.BlockSpec((N_HEADS, QB, WIN), lambda b, i, l: (0, 0, 0)),
            ],
            out_specs=pl.BlockSpec((QB, width), lambda b, i, l: (b * nq + i, 0)),
            scratch_shapes=[pltpu.VMEM((QB, far_max), F32), pltpu.VMEM((QB, WIN), F32)]),
        out_shape=jax.ShapeDtypeStruct((m_total, width), BF16),
        compiler_params=_cparams("parallel", "arbitrary"),
        name="dsa_prompt",
    )(lidx, cfar, projq, projq, pkv_f, pkv_b, pkv_b, pkv_b, dwin)


def _diff_block(fl, i, l_ref, cfar_ref, laminit_ref, qb_ref, kb_ref, vb_ref, dwin_ref,
                lq1_ref, lk1_ref, lq2_ref, lk2_ref, g_ref, o_ref, mfar_ref, mwin_ref):
    prev0 = pl.multiple_of(jnp.maximum(i - 1, 0) * QB, QB)
    diag0 = pl.multiple_of(i * QB, QB)
    lam_init = laminit_ref[l_ref[0]]
    lam = _lambda(lq1_ref[...], lk1_ref[...], lq2_ref[...], lk2_ref[...], lam_init)
    col_win = lax.broadcasted_iota(jnp.int32, (QB, WIN), 1)
    mwin_ref[...] = jnp.where((col_win >= QB) | (i > 0), 0.0, NEG)
    if fl:
        col_far = lax.broadcasted_iota(jnp.int32, (QB, fl), 1)
        mfar_ref[:, 0:fl] = jnp.where(col_far < (i - 1) * QB, 0.0, NEG)
    c1 = SUB_DIM ** -0.5 * LOG2E
    g = g_ref[...]

    def head(h, carry):
        off = pl.multiple_of(h * HEAD_DIM, HEAD_DIM)
        hc = pl.ds(off, HEAD_DIM)
        q1, q2 = _lane_half_masks(qb_ref[:, hc])
        k_win = _window(kb_ref, prev0, diag0, hc)
        v_win = _window(vb_ref, prev0, diag0, hc)
        bias_win = dwin_ref[h] + mwin_ref[...]
        if fl:
            k_far = kb_ref[0:fl, hc]
            shifts = [cfar_ref[h], None]
            (e1f, e1w), r1 = _softmax2([_dot_nt(q1, k_far) * c1 + mfar_ref[:, 0:fl],
                                        _dot_nt(q1, k_win) * c1 + bias_win], shifts)
            (e2f, e2w), r2 = _softmax2([_dot_nt(q2, k_far) * c1 + mfar_ref[:, 0:fl],
                                        _dot_nt(q2, k_win) * c1 + bias_win], shifts)
            r2 = r2 * lam
            o = (_dot((e1f * r1 - e2f * r2).astype(BF16), vb_ref[0:fl, hc])
                 + _dot((e1w * r1 - e2w * r2).astype(BF16), v_win))
        else:
            (e1w,), r1 = _softmax2([_dot_nt(q1, k_win) * c1 + bias_win], [None])
            (e2w,), r2 = _softmax2([_dot_nt(q2, k_win) * c1 + bias_win], [None])
            r2 = r2 * lam
            o = _dot((e1w * r1 - e2w * r2).astype(BF16), v_win)
        o_ref[:, hc] = _sub_ln(o, g, lam_init).astype(o_ref.dtype)
        return carry

    lax.fori_loop(0, N_HEADS, head, 0, unroll=HEAD_UNROLL)


def _diff_prompt_body(buckets, l_ref, *refs):
    i = pl.program_id(1)
    for lo, hi, fl in buckets:
        pl.when((i >= lo) & (i < hi))(functools.partial(_diff_block, fl, i, l_ref, *refs))


def _diff_prompt(lidx, projq, pkv_b, dwin, cfar, lam_inits, lq1, lk1, lq2, lk2, subg, batch, seq, m_total, cols):
    nq = seq // QB
    far_max = seq - WIN
    width = N_HEADS * HEAD_DIM
    vec = lambda n: pl.BlockSpec((None, 1, n), lambda b, i, l: (l[0], 0, 0))
    return pl.pallas_call(
        functools.partial(_diff_prompt_body, _far_buckets(nq, far_max)),
        grid_spec=pltpu.PrefetchScalarGridSpec(
            num_scalar_prefetch=1, grid=(batch, nq),
            in_specs=[
                pl.BlockSpec(memory_space=pltpu.SMEM),
                pl.BlockSpec(memory_space=pltpu.SMEM),
                pl.BlockSpec((QB, width), lambda b, i, l: (b * nq + i, cols["qb"])),
                pl.BlockSpec((seq, width), lambda b, i, l: (b, cols["kb"])),
                pl.BlockSpec((seq, width), lambda b, i, l: (b, cols["vb"])),
                pl.BlockSpec((N_HEADS, QB, WIN), lambda b, i, l: (0, 0, 0)),
                vec(SUB_DIM), vec(SUB_DIM), vec(SUB_DIM), vec(SUB_DIM), vec(HEAD_DIM),
            ],
            out_specs=pl.BlockSpec((QB, width), lambda b, i, l: (b * nq + i, 0)),
            scratch_shapes=[pltpu.VMEM((QB, far_max), F32), pltpu.VMEM((QB, WIN), F32)]),
        out_shape=jax.ShapeDtypeStruct((m_total, width), BF16),
        compiler_params=_cparams("parallel", "arbitrary"),
        name="diff_prompt",
    )(lidx, cfar, lam_inits, projq, pkv_b, pkv_b, dwin, lq1, lk1, lq2, lk2, subg)


def _dsa_sample_body(topk, l_ref, qa_ref, qi_ref, wi_ref, kn_ref, vn_ref, kxn_ref, kc_ref, vc_ref, kxc_ref,
                     bc_ref, bn_ref, oin_ref, o_ref, mc_ref, mn_ref):
    s_c, s_n = _index_scores(qi_ref[...], wi_ref[...], [kxc_ref[...].astype(BF16), kxn_ref[...]])
    key_c = _sortable(s_c)
    key_n = _sortable(s_n)
    thr = _kth_largest([key_c, key_n], topk)
    mc_ref[...] = jnp.where(key_c >= thr, 0.0, NEG)
    mn_ref[...] = jnp.where(key_n >= thr, 0.0, NEG)
    k_c = kc_ref[...].astype(BF16)
    v_c = vc_ref[...].astype(BF16)
    k_n = kn_ref[...]
    v_n = vn_ref[...]
    c1 = HEAD_DIM ** -0.5 * LOG2E

    def head(h, carry):
        off = pl.multiple_of(h * HEAD_DIM, HEAD_DIM)
        q = qa_ref[:, pl.ds(off, HEAD_DIM)]
        tc = _dot_nt(q, k_c) * c1 + (bc_ref[h] + mc_ref[...])
        tn = _dot_nt(q, k_n) * c1 + (bn_ref[h] + mn_ref[...])
        (ec, en), rl = _softmax2([tc, tn], [None, None])
        o = (_dot(ec.astype(BF16), v_c) + _dot(en.astype(BF16), v_n)) * rl
        o_ref[:, pl.ds(off, HEAD_DIM)] = o.astype(o_ref.dtype)
        return carry

    lax.fori_loop(0, N_HEADS, head, 0)


def _dsa_sample(lidx, projq, pkv_b, pkv_f, cache_ka, cache_va, cache_kx2, bias_c, bias_n, o_all, row0, cols):
    _, nb, past, _ = cache_ka.shape
    t = bias_c.shape[1]
    r0 = row0 // t
    topk = min(TOPK_MAX, (past + t) // 4)
    width = N_HEADS * HEAD_DIM
    w128 = lambda key: (lambda b, l: (r0 + b, cols[key]))
    cache = lambda n: pl.BlockSpec((None, None, past, n), lambda b, l: (l[0], b, 0, 0))
    return pl.pallas_call(
        functools.partial(_dsa_sample_body, topk),
        grid_spec=pltpu.PrefetchScalarGridSpec(
            num_scalar_prefetch=1, grid=(nb,),
            in_specs=[
                pl.BlockSpec((t, width), w128("qa")),
                pl.BlockSpec((t, N_IDX_HEADS * IDX_DIM), w128("qi")),
                pl.BlockSpec((t, LANES), w128("wi")),
                pl.BlockSpec((t, HEAD_DIM), w128("ka")),
                pl.BlockSpec((t, HEAD_DIM), w128("va")),
                pl.BlockSpec((t, LANES), w128("kx")),
                cache(HEAD_DIM), cache(HEAD_DIM), cache(LANES),
                pl.BlockSpec((N_HEADS, t, past), lambda b, l: (0, 0, 0)),
                pl.BlockSpec((N_HEADS, t, t), lambda b, l: (0, 0, 0)),
                pl.BlockSpec(memory_space=pl.ANY),
            ],
            out_specs=pl.BlockSpec((t, width), lambda b, l: (r0 + b, 0)),
            scratch_shapes=[pltpu.VMEM((t, past), F32), pltpu.VMEM((t, t), F32)]),
        out_shape=jax.ShapeDtypeStruct(o_all.shape, BF16),
        input_output_aliases={12: 0},
        compiler_params=_cparams("arbitrary"),
        name="dsa_sample",
    )(lidx, projq, projq, pkv_f, pkv_b, pkv_b, pkv_b, cache_ka, cache_va, cache_kx2, bias_c, bias_n, o_all)


def _diff_sample_body(l_ref, laminit_ref, qb_ref, kn_ref, vn_ref, kc_ref, vc_ref, bc_ref, bn_ref,
                      lq1_ref, lk1_ref, lq2_ref, lk2_ref, g_ref, oin_ref, o_ref):
    lam_init = laminit_ref[l_ref[0]]
    lam = _lambda(lq1_ref[...], lk1_ref[...], lq2_ref[...], lk2_ref[...], lam_init)
    c1 = SUB_DIM ** -0.5 * LOG2E
    g = g_ref[...]

    def head(h, carry):
        off = pl.multiple_of(h * HEAD_DIM, HEAD_DIM)
        hc = pl.ds(off, HEAD_DIM)
        q1, q2 = _lane_half_masks(qb_ref[:, hc])
        k_c = kc_ref[:, hc].astype(BF16)
        v_c = vc_ref[:, hc].astype(BF16)
        k_n = kn_ref[:, hc]
        v_n = vn_ref[:, hc]
        bc = bc_ref[h]
        bn = bn_ref[h]
        none2 = [None, None]
        (e1c, e1n), r1 = _softmax2([_dot_nt(q1, k_c) * c1 + bc, _dot_nt(q1, k_n) * c1 + bn], none2)
        (e2c, e2n), r2 = _softmax2([_dot_nt(q2, k_c) * c1 + bc, _dot_nt(q2, k_n) * c1 + bn], none2)
        r2 = r2 * lam
        o = (_dot((e1c * r1 - e2c * r2).astype(BF16), v_c) + _dot((e1n * r1 - e2n * r2).astype(BF16), v_n))
        o_ref[:, hc] = _sub_ln(o, g, lam_init).astype(o_ref.dtype)
        return carry

    lax.fori_loop(0, N_HEADS, head, 0)


def _diff_sample(lidx, projq, pkv_b, cache_kb, cache_vb, bias_c, bias_n, lam_inits, lq1, lk1, lq2, lk2, subg,
                 o_all, row0, cols):
    _, nb, past, width = cache_kb.shape
    t = bias_c.shape[1]
    r0 = row0 // t
    vec = lambda n: pl.BlockSpec((None, 1, n), lambda b, l: (l[0], 0, 0))
    cache = pl.BlockSpec((None, None, past, width), lambda b, l: (l[0], b, 0, 0))
    return pl.pallas_call(
        _diff_sample_body,
        grid_spec=pltpu.PrefetchScalarGridSpec(
            num_scalar_prefetch=1, grid=(nb,),
            in_specs=[
                pl.BlockSpec(memory_space=pltpu.SMEM),
                pl.BlockSpec((t, width), lambda b, l: (r0 + b, cols["qb"])),
                pl.BlockSpec((t, width), lambda b, l: (r0 + b, cols["kb"])),
                pl.BlockSpec((t, width), lambda b, l: (r0 + b, cols["vb"])),
                cache, cache,
                pl.BlockSpec((N_HEADS, t, past), lambda b, l: (0, 0, 0)),
                pl.BlockSpec((N_HEADS, t, t), lambda b, l: (0, 0, 0)),
                vec(SUB_DIM), vec(SUB_DIM), vec(SUB_DIM), vec(SUB_DIM), vec(HEAD_DIM),
                pl.BlockSpec(memory_space=pl.ANY),
            ],
            out_specs=pl.BlockSpec((t, width), lambda b, l: (r0 + b, 0))),
        out_shape=jax.ShapeDtypeStruct(o_all.shape, BF16),
        input_output_aliases={14: 0},
        compiler_params=_cparams("arbitrary"),
        name="diff_sample",
    )(lidx, lam_inits, projq, pkv_b, pkv_b, cache_kb, cache_vb, bias_c, bias_n, lq1, lk1, lq2, lk2, subg, o_all)


def _layer_norm_rows(z, g, b):
    mu = jnp.mean(z, axis=-1, keepdims=True)
    zc = z - mu
    var = jnp.mean(zc * zc, axis=-1, keepdims=True)
    return zc * lax.rsqrt(var + LN_EPS) * g + b


def _route(rl):
    lane = lax.broadcasted_iota(jnp.int32, rl.shape, 1).astype(F32)
    big = 1e6
    gl = jnp.where(lane < N_GROUPS, rl, NEG)
    gmax = gl.max(axis=-1, keepdims=True)
    gsum = jnp.sum(jnp.exp(gl - gmax), axis=-1, keepdims=True)
    g_val = 1.0 / gsum
    g_idx = jnp.min(jnp.where(gl == gmax, lane, big), axis=-1, keepdims=True)
    lo = N_GROUPS + g_idx * EXPERTS_PER_GROUP
    el = jnp.where((lane >= lo) & (lane < lo + EXPERTS_PER_GROUP), rl, NEG)
    m1 = el.max(axis=-1, keepdims=True)
    i1 = jnp.min(jnp.where(el == m1, lane, big), axis=-1, keepdims=True)
    el2 = jnp.where(lane == i1, NEG, el)
    m2 = el2.max(axis=-1, keepdims=True)
    i2 = jnp.min(jnp.where(el2 == m2, lane, big), axis=-1, keepdims=True)
    e2 = jnp.exp(m2 - m1)
    den = 1.0 / (1.0 + e2)
    w1 = den * g_val
    w2 = e2 * den * g_val
    out = jnp.where(lane == 0, i1 - N_GROUPS, 0.0)
    out = jnp.where(lane == 1, i2 - N_GROUPS, out)
    out = jnp.where(lane == 2, w1, out)
    out = jnp.where(lane == 3, w2, out)
    return out


HI16 = -65536


def _pack_bf16_pair(x):
    n = x.shape[1] // 2
    bits = pltpu.bitcast(x.astype(BF16).astype(F32), jnp.int32)
    return lax.shift_right_logical(bits[:, :n], 16) | (bits[:, n:] & HI16)


def _unpack_bf16_pair(p):
    return pltpu.bitcast(lax.shift_left(p, 16), F32), pltpu.bitcast(p & HI16, F32)


def _out_ln_body(alpha, tn, rows, l_ref, oa_ref, ob_ref, wa_ref, wb_ref, x_ref, g_ref, b_ref, wr_ref,
                 h_ref, hp_ref, rt_ref):
    j = pl.program_id(1)
    acc = _dot(oa_ref[...], wa_ref[...]) + _dot(ob_ref[...], wb_ref[...])
    off = pl.multiple_of(j * tn, tn)
    h_ref[:, pl.ds(off, tn)] = alpha * x_ref[...] + acc

    @pl.when(j == pl.num_programs(1) - 1)
    def _():
        g = g_ref[...]
        b = b_ref[...]

        def chunk(c, carry):
            r = pl.multiple_of(c * rows, rows)
            hn = _layer_norm_rows(h_ref[pl.ds(r, rows), :], g, b)
            h_ref[pl.ds(r, rows), :] = hn
            hp_ref[pl.ds(r, rows), :] = _pack_bf16_pair(hn)
            rt_ref[pl.ds(r, rows), :] = _dot(hn.astype(BF16), wr_ref[...])
            return carry

        lax.fori_loop(0, h_ref.shape[0] // rows, chunk, 0)
        rt_ref[...] = _route(rt_ref[...])


def _out_ln(lidx, oa, ob, w_out, x, g, b, wr, alpha, tm, tn):
    m, half = oa.shape
    d = w_out.shape[-1]
    rows = next(r for r in (64, 48, 32, 16, 8) if tm % r == 0)
    return pl.pallas_call(
        functools.partial(_out_ln_body, alpha, tn, rows),
        grid_spec=pltpu.PrefetchScalarGridSpec(
            num_scalar_prefetch=1, grid=(m // tm, d // tn),
            in_specs=[
                pl.BlockSpec((tm, half), lambda i, j, l: (i, 0)),
                pl.BlockSpec((tm, half), lambda i, j, l: (i, 0)),
                pl.BlockSpec((None, half, tn), lambda i, j, l: (l[0], 0, j)),
                pl.BlockSpec((None, half, tn), lambda i, j, l: (l[0], 1, j)),
                pl.BlockSpec((tm, tn), lambda i, j, l: (i, j)),
                pl.BlockSpec((None, 1, d), lambda i, j, l: (l[0], 0, 0)),
                pl.BlockSpec((None, 1, d), lambda i, j, l: (l[0], 0, 0)),
                pl.BlockSpec((None, d, LANES), lambda i, j, l: (l[0], 0, 0)),
            ],
            out_specs=[pl.BlockSpec((tm, d), lambda i, j, l: (i, 0)),
                       pl.BlockSpec((tm, d // 2), lambda i, j, l: (i, 0)),
                       pl.BlockSpec((tm, LANES), lambda i, j, l: (i, 0))]),
        out_shape=[jax.ShapeDtypeStruct((m, d), F32), jax.ShapeDtypeStruct((m, d // 2), jnp.int32),
                   jax.ShapeDtypeStruct((m, LANES), F32)],
        compiler_params=_cparams("parallel", "arbitrary"),
        name="out_proj_ln_route",
    )(lidx, oa, ob, w_out, w_out, x, g, b, wr)


def _moe_body(tm, spare_row, meta_ref, te_ref, src_ref, nx1_ref, nx2_ref, dst_ref, prv_ref, gate_ref, h_hbm,
              w1_ref, w3_ref, w2_ref, y_hbm, xbuf, obuf, xb, gsem, ssem):
    t = pl.program_id(0)
    n_used = meta_ref[1]
    slot = lax.rem(t, 2)
    other = 1 - slot
    g0 = lax.rem(t, 3)
    g1 = lax.rem(t + 1, 3)
    g2 = lax.rem(t + 2, 3)

    def wait_gather(s):
        pltpu.make_async_copy(h_hbm.at[pl.ds(0, tm)], xbuf.at[s], gsem.at[s]).wait()

    def wait_scatter(s):
        pltpu.make_async_copy(obuf.at[s], y_hbm.at[pl.ds(0, tm)], ssem.at[s]).wait()

    @pl.when(t == 0)
    def _():
        def row(r, carry):
            pltpu.make_async_copy(h_hbm.at[pl.ds(src_ref[0, r], 1)], xbuf.at[0, pl.ds(r, 1)], gsem.at[0]).start()
            pltpu.make_async_copy(h_hbm.at[pl.ds(nx1_ref[0, r], 1)], xbuf.at[1, pl.ds(r, 1)], gsem.at[1]).start()
            return carry
        lax.fori_loop(0, tm, row, 0, unroll=8)
        obuf[...] = jnp.zeros_like(obuf)
        pltpu.make_async_copy(obuf.at[0], y_hbm.at[pl.ds(spare_row, tm)], ssem.at[0]).start()

    @pl.when(t < n_used)
    def _():
        wait_gather(g0)
        wait_scatter(slot)
        lo, hi = _unpack_bf16_pair(xbuf[g0])
        half = lo.shape[1]
        xb[:, 0:half] = lo.astype(BF16)
        xb[:, half:2 * half] = hi.astype(BF16)
        for r in range(tm):
            pltpu.make_async_copy(h_hbm.at[pl.ds(nx2_ref[0, r], 1)], xbuf.at[g2, pl.ds(r, 1)],
                                  gsem.at[g2]).start()
        for r in range(tm):
            pltpu.make_async_copy(obuf.at[other, pl.ds(r, 1)], y_hbm.at[pl.ds(prv_ref[0, r], 1)],
                                  ssem.at[other]).start()
        x = xb[...]
        u = _dot(x, w1_ref[...])
        v = _dot(x, w3_ref[...])
        a = (u * (1.0 / (1.0 + jnp.exp(-u))) * v).astype(BF16)
        obuf[slot] = _pack_bf16_pair(_dot(a, w2_ref[...]) * gate_ref[:, 0:1])

        @pl.when(t + 1 >= n_used)
        def _():
            def row(r, carry):
                pltpu.make_async_copy(obuf.at[slot, pl.ds(r, 1)], y_hbm.at[pl.ds(dst_ref[0, r], 1)],
                                      ssem.at[slot]).start()
                return carry
            lax.fori_loop(0, tm, row, 0, unroll=8)
            wait_gather(g1)
            wait_gather(g2)
            wait_scatter(other)
            wait_scatter(slot)


def _moe(meta, tile_expert, src3, dst3, prv3, gate, hp, w1, w3, w2, tm, y_rows):
    d, f = w1.shape[-2:]
    dp = hp.shape[1]
    nt = src3.shape[0]
    spare_row = y_rows - tm
    idx_spec = lambda imap: pl.BlockSpec((None, 1, tm), imap, memory_space=pltpu.SMEM)
    wspec = lambda shape, nbuf: pl.BlockSpec((None, None) + shape, lambda t, meta, te: (meta[0], te[t], 0, 0),
                                             pipeline_mode=pl.Buffered(nbuf))
    return pl.pallas_call(
        functools.partial(_moe_body, tm, spare_row),
        grid_spec=pltpu.PrefetchScalarGridSpec(
            num_scalar_prefetch=2, grid=(nt,),
            in_specs=[
                idx_spec(lambda t, meta, te: (t, 0, 0)),
                idx_spec(lambda t, meta, te: (jnp.minimum(t + 1, nt - 1), 0, 0)),
                idx_spec(lambda t, meta, te: (jnp.minimum(t + 2, nt - 1), 0, 0)),
                idx_spec(lambda t, meta, te: (t, 0, 0)),
                idx_spec(lambda t, meta, te: (t, 0, 0)),
                pl.BlockSpec((tm, LANES), lambda t, meta, te: (t, 0)),
                pl.BlockSpec(memory_space=pl.ANY),
                wspec((d, f), 1), wspec((d, f), 1), wspec((f, d), 2),
            ],
            out_specs=pl.BlockSpec(memory_space=pl.ANY),
            scratch_shapes=[pltpu.VMEM((3, tm, dp), jnp.int32), pltpu.VMEM((2, tm, dp), jnp.int32),
                            pltpu.VMEM((tm, d), BF16),
                            pltpu.SemaphoreType.DMA((3,)), pltpu.SemaphoreType.DMA((2,))]),
        out_shape=jax.ShapeDtypeStruct((y_rows, dp), jnp.int32),
        compiler_params=_cparams("arbitrary"),
        name="moe_experts",
    )(meta, tile_expert, src3, src3, src3, dst3, prv3, gate, hp, w1, w3, w2)


def _dispatch_plan(route, tm):
    m = route.shape[0]
    npairs = 2 * m
    nt = npairs // tm + N_EXPERTS
    p = nt * tm
    e = route[:, 0:2].astype(jnp.int32).reshape(npairs)
    wgt = route[:, 2:4].reshape(npairs)
    order = jnp.argsort(e, stable=True).astype(jnp.int32)
    counts = jnp.sum((e[:, None] == jnp.arange(N_EXPERTS, dtype=jnp.int32)[None, :]).astype(jnp.int32), axis=0)
    padded = ((counts + tm - 1) // tm) * tm
    pad_end = jnp.cumsum(padded)
    pad_start = pad_end - padded
    cnt_start = jnp.cumsum(counts) - counts
    tile_start = jnp.arange(nt, dtype=jnp.int32) * tm
    tile_expert = jnp.minimum(jnp.sum((pad_end[None, :] <= tile_start[:, None]).astype(jnp.int32), axis=1),
                              N_EXPERTS - 1)
    slot = jnp.arange(p, dtype=jnp.int32)
    se = jnp.repeat(tile_expert, tm)
    rank = slot - pad_start[se]
    valid = (rank >= 0) & (rank < counts[se])
    pair = order[jnp.clip(cnt_start[se] + rank, 0, npairs - 1)]
    src = jnp.where(valid, pair // 2, 0)
    dst = jnp.where(valid, (pair % 2) * m + pair // 2, npairs + slot)
    gate = jnp.where(valid, wgt[pair], 0.0)
    n_used = (pad_end[-1] // tm).astype(jnp.int32)
    gate128 = jnp.broadcast_to(gate[:, None], (p, LANES))
    dst3 = dst.reshape(nt, 1, tm)
    spare = (npairs + p + jnp.arange(tm, dtype=jnp.int32)).reshape(1, 1, tm)
    prv3 = jnp.concatenate([spare, dst3[:-1]], axis=0)
    return src.reshape(nt, 1, tm), dst3, prv3, gate128, tile_expert.astype(jnp.int32), n_used


def _ln2_body(alpha, l_ref, h_ref, y0_ref, y1_ref, g_ref, b_ref, o_ref, ob_ref):
    lo0, hi0 = _unpack_bf16_pair(y0_ref[...])
    lo1, hi1 = _unpack_bf16_pair(y1_ref[...])
    z = alpha * h_ref[...] + jnp.concatenate([lo0 + lo1, hi0 + hi1], axis=1)
    hn = _layer_norm_rows(z, g_ref[...], b_ref[...])
    o_ref[...] = hn
    ob_ref[...] = hn.astype(BF16)


def _ln2(lidx, h, y, g, b, alpha, tm):
    m, d = h.shape
    nblk = m // tm
    return pl.pallas_call(
        functools.partial(_ln2_body, alpha),
        grid_spec=pltpu.PrefetchScalarGridSpec(
            num_scalar_prefetch=1, grid=(nblk,),
            in_specs=[
                pl.BlockSpec((tm, d), lambda i, l: (i, 0)),
                pl.BlockSpec((tm, d // 2), lambda i, l: (i, 0)),
                pl.BlockSpec((tm, d // 2), lambda i, l: (i + nblk, 0)),
                pl.BlockSpec((None, 1, d), lambda i, l: (l[0], 0, 0)),
                pl.BlockSpec((None, 1, d), lambda i, l: (l[0], 0, 0)),
            ],
            out_specs=[pl.BlockSpec((tm, d), lambda i, l: (i, 0)),
                       pl.BlockSpec((tm, d), lambda i, l: (i, 0))]),
        out_shape=[jax.ShapeDtypeStruct((m, d), F32), jax.ShapeDtypeStruct((m, d), BF16)],
        compiler_params=_cparams("parallel"),
        name="moe_combine_ln",
    )(lidx, h, y, y, g, b)


def _pick_tile(m, candidates):
    for c in candidates:
        if m % c == 0:
            return c
    raise ValueError(f"no row tile for {m} rows")


def kernel(x_prompt, x_sample, cache_ka, cache_va, cache_kidx, cache_kb, cache_vb, rel_bias, w_in, w_out,
           lambda_q1, lambda_k1, lambda_q2, lambda_k2, subln_g, ln1_g, ln1_b, ln2_g, ln2_b,
           w_route_group, w_route_expert, w1, w3, w2):
    batch, seq, d = x_prompt.shape
    dec_batch, dec_seq, _ = x_sample.shape
    depth = w_in.shape[0]
    past = cache_ka.shape[2]
    mp = batch * seq
    ms = dec_batch * dec_seq
    m = mp + ms
    alpha = (2.0 * depth) ** 0.25
    width = N_HEADS * HEAD_DIM
    assert seq % QB == 0 and seq >= 2 * WIN and mp % dec_seq == 0 and width == 2048

    in_widths = (width, HEAD_DIM, HEAD_DIM, N_IDX_HEADS * IDX_DIM, IDX_DIM, N_IDX_HEADS, width, width, width)
    splits = [int(s) for s in np.cumsum(in_widths)[:-1]]
    wqa, wka, wva, wqi, wki, wwi, wqb, wkb, wvb = jnp.split(w_in.astype(BF16), splits, axis=-1)
    w_q = jnp.concatenate([wqa, wqb, wqi], axis=-1)
    pad = jnp.zeros(wwi.shape[:-1] + (LANES - N_IDX_HEADS,), BF16)
    w_kv = jnp.concatenate([wkb, wvb, wka, wva, wki, wki, wwi, pad], axis=-1)
    n_kv = w_kv.shape[-1]
    cols_q = {"qa": 0, "qb": 1, "qi": 4}
    cols_kv = {"kb": 0, "vb": 1, "ka": 32, "va": 33, "kx": 34, "wi": 35}
    cols = {**cols_q, **cols_kv}
    w_out_b = w_out.astype(BF16)
    w1_b = w1.astype(BF16)
    w3_b = w3.astype(BF16)
    w2_b = w2.astype(BF16)
    w_r = jnp.concatenate([w_route_group, w_route_expert,
                           jnp.zeros((depth, d, LANES - N_GROUPS - N_EXPERTS), F32)], axis=-1).astype(BF16)
    vec3 = lambda a: a.reshape(depth, 1, a.shape[-1])
    lq1, lk1, lq2, lk2, subg = map(vec3, (lambda_q1, lambda_k1, lambda_q2, lambda_k2, subln_g))
    g1, b1, g2, b2 = map(vec3, (ln1_g, ln1_b, ln2_g, ln2_b))
    lam_inits = jnp.asarray([0.8 - 0.6 * math.exp(-0.3 * l) for l in range(depth)], F32)

    dwin, cfar = _prompt_bias_tiles(rel_bias)
    sbias = _sample_bias(rel_bias, past, dec_seq)
    dwin_a, dwin_b = dwin[:N_HEADS], dwin[N_HEADS:]
    cfar_a, cfar_b = cfar[:N_HEADS], cfar[N_HEADS:]
    sb_a_c, sb_a_n = sbias[:N_HEADS, :, :past], sbias[:N_HEADS, :, past:]
    sb_b_c, sb_b_n = sbias[N_HEADS:, :, :past], sbias[N_HEADS:, :, past:]
    cache_kb2 = cache_kb.astype(BF16).reshape(cache_kb.shape[:3] + (width,))
    cache_vb2 = cache_vb.astype(BF16).reshape(cache_vb.shape[:3] + (width,))
    cache_kx2 = jnp.concatenate([cache_kidx, cache_kidx], axis=-1)

    tm_proj = _pick_tile(m, (768, 512, 256, 128, 64, 32))
    tm_out = _pick_tile(m, (528, 384, 256, 128, 64, 32))
    tm_ln2 = _pick_tile(m, (256, 128, 64, 32))
    tm_moe = 256 if m >= 1024 else 32
    y_rows = 2 * m + (2 * m // tm_moe + N_EXPERTS) * tm_moe + 2 * tm_moe

    x0 = jnp.concatenate([x_prompt.reshape(mp, d), x_sample.reshape(ms, d)], axis=0)
    x, xb = x0, x0.astype(BF16)
    rows = []
    for layer in range(depth):
        lidx = jnp.full((1,), layer, jnp.int32)
        projq = _proj(lidx, xb, w_q, tm_proj, 512, False)[0]
        pkv_b, pkv_f = _proj(lidx, xb, w_kv, tm_proj, 512, True)
        oa = _dsa_prompt(lidx, projq, pkv_b, pkv_f, dwin_a, cfar_a, batch, seq, m, cols)
        ob = _diff_prompt(lidx, projq, pkv_b, dwin_b, cfar_b, lam_inits, lq1, lk1, lq2, lk2, subg,
                          batch, seq, m, cols)
        oa = _dsa_sample(lidx, projq, pkv_b, pkv_f, cache_ka, cache_va, cache_kx2, sb_a_c, sb_a_n, oa, mp, cols)
        ob = _diff_sample(lidx, projq, pkv_b, cache_kb2, cache_vb2, sb_b_c, sb_b_n, lam_inits,
                          lq1, lk1, lq2, lk2, subg, ob, mp, cols)
        h1, h1p, route = _out_ln(lidx, oa, ob, w_out_b, x, g1, b1, w_r, alpha, tm_out, 512)
        src3, dst3, prv3, gate128, tile_expert, n_used = _dispatch_plan(route, tm_moe)
        meta = jnp.stack([lidx[0], n_used])
        y = _moe(meta, tile_expert, src3, dst3, prv3, gate128, h1p, w1_b, w3_b, w2_b, tm_moe, y_rows)
        x, xb = _ln2(lidx, h1, y, g2, b2, alpha, tm_ln2)
        rows.append(pkv_f)
    y = x
    heads = (N_HEADS, 2 * SUB_DIM)

    def split(c0, n, tail):
        return (jnp.stack([r[:mp, c0:c0 + n] for r in rows]).reshape((depth, batch, seq) + tail),
                jnp.stack([r[mp:, c0:c0 + n] for r in rows]).reshape((depth, dec_batch, dec_seq) + tail))

    p_kb, s_kb = split(0, width, heads)
    p_vb, s_vb = split(width, width, heads)
    p_ka, s_ka = split(2 * width, HEAD_DIM, (HEAD_DIM,))
    p_va, s_va = split(2 * width + HEAD_DIM, HEAD_DIM, (HEAD_DIM,))
    p_kx, s_kx = split(2 * width + 2 * HEAD_DIM, IDX_DIM, (IDX_DIM,))
    y_prompt = y[:mp].reshape(batch, seq, d)
    y_sample = y[mp:].reshape(dec_batch, dec_seq, d)
    return (y_prompt, y_sample, p_ka, p_va, p_kx, p_kb, p_vb, s_ka, s_va, s_kx, s_kb, s_vb)
```
